```python
import jax
import jax.numpy as jnp
from jax import lax
import numpy as np

D_MODEL = 1024
BATCH = 8
SEQ = 4096
DEPTH = 4

N_EVEN = (DEPTH + 1) // 2
N_ODD = DEPTH // 2
D_FF = 4 * D_MODEL
RMS_EPS = 1e-6
NEG_BIG = -1e30
F_MIN = 1e-30
A_WIDTH = D_MODEL // 2
A_HEAD_DIM = 128
A_HEADS = A_WIDTH // A_HEAD_DIM
A_CHUNK = 64
B_WIDTH = D_MODEL // 2
B_BLOCKS = 8
B_BLOCK_DIM = B_WIDTH // B_BLOCKS
B_CONV = 4
RG_C = 8.0
C_HEAD_DIM = 64
C_HEADS = D_MODEL // C_HEAD_DIM
C_KV_HEADS = 4
C_GROUP = C_HEADS // C_KV_HEADS
WINDOW = 128
ROPE_THETA = 10000.0
EVEN_IN = 4 * A_WIDTH + 2 * B_WIDTH
EVEN_SPLITS = (A_WIDTH, 2 * A_WIDTH, 3 * A_WIDTH, 4 * A_WIDTH, 4 * A_WIDTH + B_WIDTH)
ODD_IN = (C_HEADS + 2 * C_KV_HEADS) * C_HEAD_DIM
ODD_SPLITS = (C_HEADS * C_HEAD_DIM, (C_HEADS + C_KV_HEADS) * C_HEAD_DIM)

kernel_name = 'hybrid_hgrn2_rglru_swa_sinks_trunk'


def _rmsnorm(x, gain):
    xf = x.astype(jnp.float32)
    y = xf * lax.rsqrt(jnp.mean(xf * xf, axis=-1, keepdims=True) + RMS_EPS)
    return (y * gain.astype(jnp.float32)).astype(x.dtype)


def _headnorm(x, gain):
    xf = x.astype(jnp.float32)
    y = xf * lax.rsqrt(jnp.mean(xf * xf, axis=-1, keepdims=True) + RMS_EPS)
    return y * gain.astype(jnp.float32)


def _hgrn2(q, fx, v, lb):
    bsz, s = q.shape[0], q.shape[1]
    lb = lb.astype(jnp.float32)
    fx = fx.astype(jnp.float32)
    f = lb + (1.0 - lb) * jax.nn.sigmoid(fx)
    log_f = jnp.log(jnp.maximum(f, F_MIN))
    k = (1.0 - lb) * jax.nn.sigmoid(-fx)
    nc = s // A_CHUNK

    def to_chunks(t):
        return t.astype(jnp.float32).reshape(bsz, nc, A_CHUNK, A_HEADS, A_HEAD_DIM).transpose(1, 0, 3, 2, 4)

    causal = jnp.tril(jnp.ones((A_CHUNK, A_CHUNK), dtype=bool))[:, :, None]

    def step(state, inp):
        qc, kc, gc, vc = inp
        b = jnp.cumsum(gc, axis=2)
        o_inter = jnp.einsum('bhtd,bhde->bhte', qc * jnp.exp(b), state)
        diff = b[:, :, :, None, :] - b[:, :, None, :, :]
        decay = jnp.exp(jnp.where(causal, diff, NEG_BIG))
        scores = jnp.einsum('bhtsd,bhsd->bhts', qc[:, :, :, None, :] * decay, kc)
        o = o_inter + jnp.einsum('bhts,bhse->bhte', scores, vc)
        b_last = b[:, :, -1, :]
        state = jnp.exp(b_last)[..., None] * state + jnp.einsum(
            'bhsd,bhse->bhde', kc * jnp.exp(b_last[:, :, None, :] - b), vc)
        return state, o

    s0 = jnp.zeros((bsz, A_HEADS, A_HEAD_DIM, A_HEAD_DIM), jnp.float32)
    _, o = lax.scan(step, s0, (to_chunks(q), to_chunks(k), to_chunks(log_f), to_chunks(v)))
    return o.transpose(1, 0, 3, 2, 4).reshape(bsz, s, A_HEADS, A_HEAD_DIM)


def _rglru(xb, conv_w, conv_b, wa, ba, wx, bx, lam):
    bsz, s, _ = xb.shape
    xc = lax.conv_general_dilated(
        xb, conv_w[:, None, :].astype(xb.dtype), window_strides=(1,),
        padding=[(B_CONV - 1, 0)], dimension_numbers=('NWC', 'WIO', 'NWC'),
        feature_group_count=B_WIDTH) + conv_b.astype(xb.dtype)
    xf = xc.astype(jnp.float32)
    xblk = xf.reshape(bsz, s, B_BLOCKS, B_BLOCK_DIM)
    r = jax.nn.sigmoid(jnp.einsum('bsni,nij->bsnj', xblk, wa.astype(jnp.float32)).reshape(bsz, s, B_WIDTH)
                       + ba.astype(jnp.float32))
    i = jax.nn.sigmoid(jnp.einsum('bsni,nij->bsnj', xblk, wx.astype(jnp.float32)).reshape(bsz, s, B_WIDTH)
                       + bx.astype(jnp.float32))
    log_a = -RG_C * jax.nn.softplus(-lam.astype(jnp.float32)) * r
    a = jnp.exp(log_a)
    u = jnp.sqrt(jnp.maximum(-jnp.expm1(2.0 * log_a), 0.0)) * (i * xf)

    def combine(left, right):
        a1, u1 = left
        a2, u2 = right
        return a1 * a2, a2 * u1 + u2

    _, h = lax.associative_scan(combine, (a, u), axis=1)
    return h


def _rope(x, cos, sin):
    x1, x2 = jnp.split(x, 2, axis=-1)
    return jnp.concatenate([x1 * cos - x2 * sin, x2 * cos + x1 * sin], axis=-1)


def _swa_sinks(q, k, v, sinks):
    bsz, s = q.shape[0], q.shape[1]
    nb = s // WINDOW
    qb = q.reshape(bsz, nb, WINDOW, C_KV_HEADS, C_GROUP, C_HEAD_DIM)

    def band(t):
        tb = t.reshape(bsz, nb, WINDOW, C_KV_HEADS, C_HEAD_DIM)
        prev = jnp.pad(tb[:, :-1], ((0, 0), (1, 0), (0, 0), (0, 0), (0, 0)))
        return jnp.concatenate([prev, tb], axis=2)

    kb, vb = band(k), band(v)
    scores = jnp.einsum('bnqhgd,bnkhd->bnhgqk', qb, kb) * (C_HEAD_DIM ** -0.5)
    qi = jnp.arange(WINDOW)[:, None]
    ki = jnp.arange(2 * WINDOW)[None, :]
    rel = qi + WINDOW - ki
    in_band = (rel >= 0) & (rel < WINDOW)
    valid = (jnp.arange(nb) > 0)[:, None, None] | (ki >= WINDOW)[None]
    mask = in_band[None] & valid
    scores = jnp.where(mask[None, :, None, None], scores, NEG_BIG)
    sink = sinks.astype(jnp.float32).reshape(1, 1, C_KV_HEADS, C_GROUP, 1, 1)
    m = jnp.maximum(jnp.max(scores, axis=-1, keepdims=True), sink)
    p = jnp.exp(scores - m)
    denom = jnp.sum(p, axis=-1, keepdims=True) + jnp.exp(sink - m)
    out = jnp.einsum('bnhgqk,bnkhd->bnqhgd', p / denom, vb)
    return out.reshape(bsz, s, C_HEADS * C_HEAD_DIM)


def setup_inputs(seed: int = 0) -> dict:
    key = jax.random.key(seed)
    ks = jax.random.split(key, 24)
    f32 = jnp.float32

    def w(k, shape, fan_in):
        return jax.random.normal(k, shape, f32) * (fan_in ** -0.5)

    def gain(k, shape):
        return 1.0 + 0.02 * jax.random.normal(k, shape, f32)

    def bias(k, shape):
        return 0.01 * jax.random.normal(k, shape, f32)

    x = jax.random.normal(ks[0], (BATCH, SEQ, D_MODEL), f32)
    offsets = jax.random.randint(ks[1], (BATCH, 1), 0, 4096, dtype=jnp.int32)
    positions = offsets + jnp.arange(SEQ, dtype=jnp.int32)[None, :]
    a_init = jax.random.uniform(ks[17], (N_EVEN, B_WIDTH), f32, minval=0.9, maxval=0.999)
    return {
        'x': x,
        'positions': positions,
        'norm_mix': gain(ks[2], (DEPTH, D_MODEL)),
        'norm_mlp': gain(ks[3], (DEPTH, D_MODEL)),
        'w_mlp_in': w(ks[4], (DEPTH, D_MODEL, D_FF), D_MODEL),
        'w_mlp_out': w(ks[5], (DEPTH, D_FF, D_MODEL), D_FF),
        'w_in_even': w(ks[6], (N_EVEN, D_MODEL, EVEN_IN), D_MODEL),
        'w_out_even': w(ks[7], (N_EVEN, A_WIDTH + B_WIDTH, D_MODEL), A_WIDTH + B_WIDTH),
        'hgrn_lb_logits': 0.1 * jax.random.normal(ks[8], (N_EVEN, A_WIDTH), f32),
        'hgrn_out_norm': gain(ks[9], (N_EVEN, A_WIDTH)),
        'conv_w': w(ks[10], (N_EVEN, B_CONV, B_WIDTH), B_CONV),
        'conv_b': bias(ks[11], (N_EVEN, B_WIDTH)),
        'rg_wa': w(ks[12], (N_EVEN, B_BLOCKS, B_BLOCK_DIM, B_BLOCK_DIM), B_BLOCK_DIM),
        'rg_ba': bias(ks[13], (N_EVEN, B_WIDTH)),
        'rg_wx': w(ks[14], (N_EVEN, B_BLOCKS, B_BLOCK_DIM, B_BLOCK_DIM), B_BLOCK_DIM),
        'rg_bx': bias(ks[15], (N_EVEN, B_WIDTH)),
        'rg_lambda': jnp.log(a_init) - jnp.log1p(-a_init),
        'w_in_odd': w(ks[16], (N_ODD, D_MODEL, ODD_IN), D_MODEL),
        'w_out_odd': w(ks[18], (N_ODD, C_HEADS * C_HEAD_DIM, D_MODEL), C_HEADS * C_HEAD_DIM),
        'q_norm': gain(ks[19], (N_ODD, C_HEAD_DIM)),
        'k_norm': gain(ks[20], (N_ODD, C_HEAD_DIM)),
        'sinks': 0.5 * jax.random.normal(ks[21], (N_ODD, C_HEADS), f32),
    }


def reference(x, positions, norm_mix, norm_mlp, w_mlp_in, w_mlp_out, w_in_even, w_out_even,
              hgrn_lb_logits, hgrn_out_norm, conv_w, conv_b, rg_wa, rg_ba, rg_wx, rg_bx, rg_lambda,
              w_in_odd, w_out_odd, q_norm, k_norm, sinks):
    bsz, s, _ = x.shape
    sm = jax.nn.softmax(hgrn_lb_logits.astype(jnp.float32), axis=0)
    lower_bounds = jnp.cumsum(sm, axis=0) - sm[0]
    inv_freq = ROPE_THETA ** (-jnp.arange(0, C_HEAD_DIM, 2, dtype=jnp.float32) / C_HEAD_DIM)
    ang = positions.astype(jnp.float32)[..., None] * inv_freq
    cos = jnp.cos(ang)[:, :, None, :]
    sin = jnp.sin(ang)[:, :, None, :]

    for layer in range(DEPTH):
        h = _rmsnorm(x, norm_mix[layer])
        if layer % 2 == 0:
            e = layer // 2
            z = h @ w_in_even[e]
            qa, fa, va, ga, gate_b, xb = jnp.split(z, EVEN_SPLITS, axis=-1)
            heads = lambda t: t.reshape(bsz, s, A_HEADS, A_HEAD_DIM)
            oa = _hgrn2(heads(qa), heads(fa), heads(va), lower_bounds[e].reshape(A_HEADS, A_HEAD_DIM))
            oa = _headnorm(oa, hgrn_out_norm[e].reshape(A_HEADS, A_HEAD_DIM)).reshape(bsz, s, A_WIDTH)
            ya = (oa * jax.nn.silu(ga.astype(jnp.float32))).astype(x.dtype)
            hb = _rglru(xb, conv_w[e], conv_b[e], rg_wa[e], rg_ba[e], rg_wx[e], rg_bx[e], rg_lambda[e])
            yb = (hb * jax.nn.gelu(gate_b.astype(jnp.float32))).astype(x.dtype)
            mix = jnp.concatenate([ya, yb], axis=-1) @ w_out_even[e]
        else:
            o = layer // 2
            z = h @ w_in_odd[o]
            q, k, v = jnp.split(z, ODD_SPLITS, axis=-1)
            q = _rope(_headnorm(q.reshape(bsz, s, C_HEADS, C_HEAD_DIM), q_norm[o]), cos, sin)
            k = _rope(_headnorm(k.reshape(bsz, s, C_KV_HEADS, C_HEAD_DIM), k_norm[o]), cos, sin)
            v = v.reshape(bsz, s, C_KV_HEADS, C_HEAD_DIM).astype(jnp.float32)
            attn = _swa_sinks(q, k, v, sinks[o])
            mix = attn.astype(x.dtype) @ w_out_odd[o]
        x = x + mix
        h = _rmsnorm(x, norm_mlp[layer])
        x = x + jnp.square(jax.nn.relu(h @ w_mlp_in[layer])) @ w_mlp_out[layer]
    return x
```

```python
import functools

import jax
import jax.numpy as jnp
from jax import lax
from jax.experimental import pallas as pl
from jax.experimental.pallas import tpu as pltpu

F32 = jnp.float32
BF16 = jnp.bfloat16

D_MODEL = 1024
D_FF = 4 * D_MODEL
RMS_EPS = 1e-6
NEG_BIG = -1e30
F_MIN = 1e-30
A_WIDTH = D_MODEL // 2
A_HEAD_DIM = 128
A_HEADS = A_WIDTH // A_HEAD_DIM
B_WIDTH = D_MODEL // 2
B_BLOCKS = 8
B_BLOCK_DIM = B_WIDTH // B_BLOCKS
B_CONV = 4
RG_C = 8.0
C_HEAD_DIM = 64
C_HEADS = D_MODEL // C_HEAD_DIM
C_KV_HEADS = 4
C_GROUP = C_HEADS // C_KV_HEADS
WINDOW = 128
ROPE_THETA = 10000.0
EVEN_IN = 4 * A_WIDTH + 2 * B_WIDTH
ODD_IN = (C_HEADS + 2 * C_KV_HEADS) * C_HEAD_DIM

LANES = 128
SUBLANES = 8
VMEM_LIMIT_BYTES = 56 * 1024 * 1024

TM_PROJ = 512
TM_MLP = 512
FF_CHUNK = 1024
T_HGRN = 256
T_RGLRU = 256
T_ATTN = 512
T_ROPE = 1024


def _params(semantics):
    return pltpu.CompilerParams(dimension_semantics=semantics, vmem_limit_bytes=VMEM_LIMIT_BYTES)


def _rmsnorm(xf, gain):
    ms = jnp.mean(xf * xf, axis=-1, keepdims=True)
    return xf * lax.rsqrt(ms + RMS_EPS) * gain


def _sigmoid_pair(x):
    e = jnp.exp(-jnp.abs(x))
    den = 1.0 + e
    big = 1.0 / den
    small = e / den
    pos = x >= 0
    return jnp.where(pos, big, small), jnp.where(pos, small, big)


def _sigmoid(x):
    return _sigmoid_pair(x)[0]


def _inproj_kernel(x_ref, g_ref, w_ref, o_ref, *, chunk):
    h = _rmsnorm(x_ref[...], g_ref[...]).astype(BF16)
    n_out = o_ref.shape[1]
    for c in range(n_out // chunk):
        cols = slice(c * chunk, (c + 1) * chunk)
        o_ref[:, cols] = jnp.dot(h, w_ref[:, cols], preferred_element_type=F32)


def _inproj(x2, gain, w_bf16):
    n, d = x2.shape
    n_out = w_bf16.shape[1]
    chunk = 512
    return pl.pallas_call(
        functools.partial(_inproj_kernel, chunk=chunk),
        out_shape=jax.ShapeDtypeStruct((n, n_out), F32),
        grid=(n // TM_PROJ,),
        in_specs=[
            pl.BlockSpec((TM_PROJ, d), lambda i: (i, 0)),
            pl.BlockSpec((1, d), lambda i: (0, 0)),
            pl.BlockSpec((d, n_out), lambda i: (0, 0)),
        ],
        out_specs=pl.BlockSpec((TM_PROJ, n_out), lambda i: (i, 0)),
        compiler_params=_params(("parallel",)),
        name="inproj",
    )(x2, gain, w_bf16)


def _mlp_kernel(*refs, n_mix):
    x_ref = refs[0]
    mix_refs = refs[1:1 + n_mix]
    wo_refs = refs[1 + n_mix:1 + 2 * n_mix]
    g_ref, w1_ref, w2_ref, o_ref, act_ref = refs[1 + 2 * n_mix:]
    x1 = x_ref[...]
    for m_ref, wo_ref in zip(mix_refs, wo_refs):
        x1 = x1 + jnp.dot(m_ref[...], wo_ref[...], preferred_element_type=F32)
    o_ref[...] = x1
    h = _rmsnorm(x1, g_ref[...]).astype(BF16)
    for c in range(D_FF // FF_CHUNK):
        cols = slice(c * FF_CHUNK, (c + 1) * FF_CHUNK)
        u = jnp.maximum(jnp.dot(h, w1_ref[:, cols], preferred_element_type=F32), 0.0)
        act_ref[:, cols] = (u * u).astype(BF16)
    o_ref[...] += jnp.dot(act_ref[...], w2_ref[...], preferred_element_type=F32)


def _mlp(x2, mixes, w_out_bf16, gain, w1_bf16, w2_bf16):
    n, d = x2.shape
    n_mix = len(mixes)
    row = lambda i: (i, 0)
    const = lambda i: (0, 0)
    in_specs = [pl.BlockSpec((TM_MLP, d), row)]
    in_specs += [pl.BlockSpec((TM_MLP, m.shape[1]), row) for m in mixes]
    k = mixes[0].shape[1]
    in_specs += [pl.BlockSpec((k, d), functools.partial(lambda j, i: (j, 0), j)) for j in range(n_mix)]
    in_specs += [
        pl.BlockSpec((1, d), const),
        pl.BlockSpec((d, D_FF), const),
        pl.BlockSpec((D_FF, d), const),
    ]
    return pl.pallas_call(
        functools.partial(_mlp_kernel, n_mix=n_mix),
        out_shape=jax.ShapeDtypeStruct((n, d), F32),
        grid=(n // TM_MLP,),
        in_specs=in_specs,
        out_specs=pl.BlockSpec((TM_MLP, d), row),
        scratch_shapes=[pltpu.VMEM((TM_MLP, D_FF), BF16)],
        compiler_params=_params(("parallel",)),
        name="mlp",
    )(x2, *mixes, *([w_out_bf16] * n_mix), gain, w1_bf16, w2_bf16)


def _hgrn_kernel(q_ref, f_ref, v_ref, g_ref, lbl_ref, gn_ref, o_ref, st_ref, oacc_ref, *, layer_e):
    t_rows = q_ref.shape[0]

    @pl.when(pl.program_id(1) == 0)
    def _():
        st_ref[...] = jnp.zeros_like(st_ref)

    logits = lbl_ref[...]
    ex = jnp.exp(logits - jnp.max(logits, axis=0, keepdims=True))
    sm = ex / jnp.sum(ex, axis=0, keepdims=True)
    lb_all = jnp.sum(sm[:layer_e + 1], axis=0, keepdims=True) - sm[0:1]

    row = lax.broadcasted_iota(jnp.int32, (SUBLANES, A_HEAD_DIM), 0)

    def block(i, carry):
        r = pl.multiple_of(i * SUBLANES, SUBLANES)
        rows = pl.ds(r, SUBLANES)
        for h in range(A_HEADS):
            cols = slice(h * A_HEAD_DIM, (h + 1) * A_HEAD_DIM)
            lb = lb_all[:, cols]
            q = q_ref[rows, cols]
            v = v_ref[rows, cols]
            sig, sig_neg = _sigmoid_pair(f_ref[rows, cols])
            f = lb + (1.0 - lb) * sig
            k = (1.0 - lb) * sig_neg
            b = jnp.log(jnp.maximum(f, F_MIN))
            for sh in (1, 2, 4):
                b = b + jnp.where(row >= sh, pltpu.roll(b, sh, axis=0), 0.0)
            b_end = b[SUBLANES - 1:SUBLANES, :]
            st = st_ref[h]
            q_dec = (q * jnp.exp(b)).astype(BF16)
            o = lax.dot_general(q_dec, st.astype(BF16), (((1,), (1,)), ((), ())),
                                preferred_element_type=F32)
            for s in range(SUBLANES):
                w = q * k[s:s + 1, :] * jnp.exp(jnp.minimum(b - b[s:s + 1, :], 0.0))
                a = jnp.sum(w, axis=-1, keepdims=True)
                o = o + jnp.where(row >= s, a, 0.0) * v[s:s + 1, :]
            oacc_ref[rows, cols] = o
            k_dec = (k * jnp.exp(b_end - b)).astype(BF16)
            upd = lax.dot_general(v.astype(BF16), k_dec, (((0,), (0,)), ((), ())),
                                  preferred_element_type=F32)
            st_ref[h] = st * jnp.exp(b_end) + upd
        return carry

    lax.fori_loop(0, t_rows // SUBLANES, block, 0)

    gn = gn_ref[...]
    for h in range(A_HEADS):
        cols = slice(h * A_HEAD_DIM, (h + 1) * A_HEAD_DIM)
        o = oacc_ref[:, cols]
        on = o * lax.rsqrt(jnp.mean(o * o, axis=-1, keepdims=True) + RMS_EPS) * gn[:, cols]
        g = g_ref[:, cols]
        o_ref[:, cols] = (on * (g * _sigmoid(g))).astype(o_ref.dtype)


def _hgrn(z, lb_logits, out_norm, layer_e, bsz, seq):
    n = z.shape[0]
    tpb = seq // T_HGRN
    spec = lambda c: pl.BlockSpec((T_HGRN, A_WIDTH), lambda b, t: (b * tpb + t, c))
    const = lambda b, t: (0, 0)
    return pl.pallas_call(
        functools.partial(_hgrn_kernel, layer_e=layer_e),
        out_shape=jax.ShapeDtypeStruct((n, A_WIDTH), BF16),
        grid=(bsz, tpb),
        in_specs=[spec(0), spec(1), spec(2), spec(3),
                  pl.BlockSpec(lb_logits.shape, const),
                  pl.BlockSpec((1, A_WIDTH), const)],
        out_specs=pl.BlockSpec((T_HGRN, A_WIDTH), lambda b, t: (b * tpb + t, 0)),
        scratch_shapes=[pltpu.VMEM((A_HEADS, A_HEAD_DIM, A_HEAD_DIM), F32),
                        pltpu.VMEM((T_HGRN, A_WIDTH), F32)],
        compiler_params=_params(("parallel", "arbitrary")),
        name="hgrn2",
    )(z, z, z, z, lb_logits, out_norm)


def _gelu_tanh(x):
    return x * (0.5 * (1.0 + jnp.tanh(0.7978845608028654 * (x + 0.044715 * (x * x * x)))))


def _rglru_kernel(gate_ref, xb_ref, cw_ref, cb_ref, wg_ref, bg_ref, lam_ref, o_ref,
                  xext_ref, a_ref, u_ref, h_ref):
    t_rows = xb_ref.shape[0]
    pad = SUBLANES

    @pl.when(pl.program_id(1) == 0)
    def _():
        xext_ref[0:pad, :] = jnp.zeros((pad, B_WIDTH), F32)
        h_ref[...] = jnp.zeros_like(h_ref)

    xext_ref[pad:pad + t_rows, :] = xb_ref[...]
    cw = cw_ref[...]
    xc = cb_ref[...] + cw[B_CONV - 1:B_CONV, :] * xb_ref[...]
    for j in range(B_CONV - 1):
        shift = B_CONV - 1 - j
        xc = xc + cw[j:j + 1, :] * xext_ref[pad - shift:pad - shift + t_rows, :]
    xext_ref[0:pad, :] = xext_ref[t_rows:t_rows + pad, :]

    gz = jnp.dot(xc.astype(BF16), wg_ref[...], preferred_element_type=F32) + bg_ref[...]
    r = _sigmoid(gz[:, :B_WIDTH])
    i = _sigmoid(gz[:, B_WIDTH:])
    lam = lam_ref[...]
    softplus_neg_lam = jnp.maximum(-lam, 0.0) + jnp.log1p(jnp.exp(-jnp.abs(lam)))
    a = jnp.exp((-RG_C * softplus_neg_lam) * r)
    a_ref[...] = a
    u_ref[...] = jnp.sqrt(jnp.maximum(1.0 - a * a, 0.0)) * (i * xc)

    row = lax.broadcasted_iota(jnp.int32, (SUBLANES, B_WIDTH), 0)

    def group(gidx, h_prev):
        rows = pl.ds(pl.multiple_of(gidx * SUBLANES, SUBLANES), SUBLANES)
        a_cum = a_ref[rows, :]
        h = u_ref[rows, :]
        for sh in (1, 2, 4):
            keep = row >= sh
            h = h + a_cum * jnp.where(keep, pltpu.roll(h, sh, axis=0), 0.0)
            a_cum = a_cum * jnp.where(keep, pltpu.roll(a_cum, sh, axis=0), 1.0)
        h = h + a_cum * h_prev
        u_ref[rows, :] = h
        return jnp.broadcast_to(h[SUBLANES - 1:SUBLANES, :], (SUBLANES, B_WIDTH))

    h_ref[...] = lax.fori_loop(0, t_rows // SUBLANES, group, h_ref[...])
    o_ref[...] = (u_ref[...] * _gelu_tanh(gate_ref[...])).astype(o_ref.dtype)


def _rglru(z, conv_w, conv_b, w_gates_bf16, b_gates, lam, bsz, seq):
    n = z.shape[0]
    tpb = seq // T_RGLRU
    spec = lambda c: pl.BlockSpec((T_RGLRU, B_WIDTH), lambda b, t: (b * tpb + t, c))
    const = lambda b, t: (0, 0)
    return pl.pallas_call(
        _rglru_kernel,
        out_shape=jax.ShapeDtypeStruct((n, B_WIDTH), BF16),
        grid=(bsz, tpb),
        in_specs=[spec(4), spec(5),
                  pl.BlockSpec((B_CONV, B_WIDTH), const),
                  pl.BlockSpec((1, B_WIDTH), const),
                  pl.BlockSpec((B_WIDTH, 2 * B_WIDTH), const),
                  pl.BlockSpec((1, 2 * B_WIDTH), const),
                  pl.BlockSpec((1, B_WIDTH), const)],
        out_specs=pl.BlockSpec((T_RGLRU, B_WIDTH), lambda b, t: (b * tpb + t, 0)),
        scratch_shapes=[pltpu.VMEM((T_RGLRU + 2 * SUBLANES, B_WIDTH), F32),
                        pltpu.VMEM((T_RGLRU, B_WIDTH), F32),
                        pltpu.VMEM((T_RGLRU, B_WIDTH), F32),
                        pltpu.VMEM((SUBLANES, B_WIDTH), F32)],
        compiler_params=_params(("parallel", "arbitrary")),
        name="rglru",
    )(z, z, conv_w, conv_b, w_gates_bf16, b_gates, lam)


def _rope_kernel(pos_ref, invf_ref, sign_ref, cos_ref, sin_ref):
    ang = pos_ref[...].astype(F32) * invf_ref[...]
    cos_ref[...] = jnp.cos(ang)
    sin_ref[...] = jnp.sin(ang) * sign_ref[...]


def _rope_tables(pos_col, invf_tile, sign_tile):
    n = pos_col.shape[0]
    const = lambda i: (0, 0)
    return pl.pallas_call(
        _rope_kernel,
        out_shape=(jax.ShapeDtypeStruct((n, LANES), F32), jax.ShapeDtypeStruct((n, LANES), F32)),
        grid=(n // T_ROPE,),
        in_specs=[pl.BlockSpec((T_ROPE, 1), lambda i: (i, 0)),
                  pl.BlockSpec((1, LANES), const),
                  pl.BlockSpec((1, LANES), const)],
        out_specs=(pl.BlockSpec((T_ROPE, LANES), lambda i: (i, 0)),
                   pl.BlockSpec((T_ROPE, LANES), lambda i: (i, 0))),
        compiler_params=_params(("parallel",)),
        name="rope_tables",
    )(pos_col, invf_tile, sign_tile)


def _split_bf16(x):
    hi = x.astype(BF16)
    lo = (x - hi.astype(F32)).astype(BF16)
    return hi, lo


def _attn_kernel(sinks_ref, q_ref, k_ref, v_ref, cos_ref, sin_ref, qg_ref, kg_ref, seg_ref, o_ref,
                 qlo_ref, qhi_ref, kd_ref, vd_ref):
    t_rows = q_ref.shape[0]
    n_blocks = t_rows // WINDOW
    first_tile = pl.program_id(1) == 0

    lane = lax.broadcasted_iota(jnp.int32, (t_rows, LANES), 1)
    low_half = lane < C_HEAD_DIM
    first_rot_half = (lane % C_HEAD_DIM) < (C_HEAD_DIM // 2)
    cos = cos_ref[...]
    sin = sin_ref[...]
    seg = seg_ref[...]

    def norm_rope(x, gain):
        hi, lo = _split_bf16(x * x)
        ss = (jnp.dot(hi, seg, preferred_element_type=F32)
              + jnp.dot(lo, seg, preferred_element_type=F32))
        xn = x * lax.rsqrt(ss * (1.0 / C_HEAD_DIM) + RMS_EPS) * gain
        rot = jnp.where(first_rot_half, pltpu.roll(xn, LANES - C_HEAD_DIM // 2, axis=1),
                        pltpu.roll(xn, C_HEAD_DIM // 2, axis=1))
        return xn * cos + rot * sin

    @pl.when(first_tile)
    def _():
        kd_ref[:, 0:WINDOW, :] = jnp.zeros((C_KV_HEADS, WINDOW, LANES), BF16)
        vd_ref[:, 0:WINDOW, :] = jnp.zeros((C_KV_HEADS, WINDOW, LANES), BF16)

    @pl.when(jnp.logical_not(first_tile))
    def _():
        kd_ref[:, 0:WINDOW, :] = kd_ref[:, t_rows:t_rows + WINDOW, :]
        vd_ref[:, 0:WINDOW, :] = vd_ref[:, t_rows:t_rows + WINDOW, :]

    for pair in range(C_KV_HEADS // 2):
        cols = slice(pair * LANES, (pair + 1) * LANES)
        kr = norm_rope(k_ref[:, cols], kg_ref[...])
        vr = v_ref[:, cols]
        k_sw = pltpu.roll(kr, C_HEAD_DIM, axis=1)
        v_sw = pltpu.roll(vr, C_HEAD_DIM, axis=1)
        kd_ref[2 * pair, WINDOW:, :] = jnp.where(low_half, kr, k_sw).astype(BF16)
        kd_ref[2 * pair + 1, WINDOW:, :] = jnp.where(low_half, k_sw, kr).astype(BF16)
        vd_ref[2 * pair, WINDOW:, :] = jnp.where(low_half, vr, v_sw).astype(BF16)
        vd_ref[2 * pair + 1, WINDOW:, :] = jnp.where(low_half, v_sw, vr).astype(BF16)

    scale = C_HEAD_DIM ** -0.5
    for hp in range(C_HEADS // 2):
        cols = slice(hp * LANES, (hp + 1) * LANES)
        qr = norm_rope(q_ref[:, cols], qg_ref[...]) * scale
        qlo_ref[hp] = jnp.where(low_half, qr, 0.0).astype(BF16)
        qhi_ref[hp] = jnp.where(low_half, 0.0, qr).astype(BF16)

    qi = lax.broadcasted_iota(jnp.int32, (2 * WINDOW, 2 * WINDOW), 0) % WINDOW
    ki = lax.broadcasted_iota(jnp.int32, (2 * WINDOW, 2 * WINDOW), 1)
    rel = qi + WINDOW - ki
    in_band = (rel >= 0) & (rel < WINDOW)
    own = ki >= WINDOW
    upper_rows = lax.broadcasted_iota(jnp.int32, (2 * WINDOW, 1), 0) < WINDOW
    out_low = lax.broadcasted_iota(jnp.int32, (WINDOW, LANES), 1) < C_HEAD_DIM

    def qblock(j, carry):
        r0 = pl.multiple_of(j * WINDOW, WINDOW)
        has_prev = jnp.logical_or(j > 0, jnp.logical_not(first_tile))
        mask = in_band & (own | has_prev)
        for hp in range(C_HEADS // 2):
            kvh = (2 * hp) // C_GROUP
            lhs = jnp.concatenate([qlo_ref[hp, pl.ds(r0, WINDOW), :],
                                   qhi_ref[hp, pl.ds(r0, WINDOW), :]], axis=0)
            keys = kd_ref[kvh, pl.ds(r0, 2 * WINDOW), :]
            vals = vd_ref[kvh, pl.ds(r0, 2 * WINDOW), :]
            s = lax.dot_general(lhs, keys, (((1,), (1,)), ((), ())), preferred_element_type=F32)
            s = jnp.where(mask, s, NEG_BIG)
            sink = jnp.where(upper_rows, sinks_ref[2 * hp], sinks_ref[2 * hp + 1])
            m = jnp.maximum(jnp.max(s, axis=-1, keepdims=True), sink)
            p = jnp.exp(s - m)
            denom = jnp.sum(p, axis=-1, keepdims=True) + jnp.exp(sink - m)
            pv = jnp.dot(p.astype(BF16), vals, preferred_element_type=F32) / denom
            o_ref[pl.ds(r0, WINDOW), hp * LANES:(hp + 1) * LANES] = jnp.where(
                out_low, pv[:WINDOW], pv[WINDOW:]).astype(o_ref.dtype)
        return carry

    lax.fori_loop(0, n_blocks, qblock, 0)


def _attn(z, cos_t, sin_t, q_gain_tile, k_gain_tile, seg_ones, sinks, bsz, seq):
    n = z.shape[0]
    tpb = seq // T_ATTN
    rowmap = lambda c: (lambda b, t: (b * tpb + t, c))
    const = lambda b, t: (0, 0)
    kv_w = C_KV_HEADS * C_HEAD_DIM
    return pl.pallas_call(
        _attn_kernel,
        out_shape=jax.ShapeDtypeStruct((n, D_MODEL), BF16),
        grid=(bsz, tpb),
        in_specs=[pl.BlockSpec(memory_space=pltpu.SMEM),
                  pl.BlockSpec((T_ATTN, D_MODEL), rowmap(0)),
                  pl.BlockSpec((T_ATTN, kv_w), rowmap(D_MODEL // kv_w)),
                  pl.BlockSpec((T_ATTN, kv_w), rowmap(D_MODEL // kv_w + 1)),
                  pl.BlockSpec((T_ATTN, LANES), rowmap(0)),
                  pl.BlockSpec((T_ATTN, LANES), rowmap(0)),
                  pl.BlockSpec((1, LANES), const),
                  pl.BlockSpec((1, LANES), const),
                  pl.BlockSpec((LANES, LANES), const)],
        out_specs=pl.BlockSpec((T_ATTN, D_MODEL), rowmap(0)),
        scratch_shapes=[pltpu.VMEM((C_HEADS // 2, T_ATTN, LANES), BF16),
                        pltpu.VMEM((C_HEADS // 2, T_ATTN, LANES), BF16),
                        pltpu.VMEM((C_KV_HEADS, T_ATTN + WINDOW, LANES), BF16),
                        pltpu.VMEM((C_KV_HEADS, T_ATTN + WINDOW, LANES), BF16)],
        compiler_params=_params(("parallel", "arbitrary")),
        name="swa",
    )(sinks, z, z, z, cos_t, sin_t, q_gain_tile, k_gain_tile, seg_ones)


def _block_diag(w):
    nb, bi, bj = w.shape
    eye = jnp.eye(nb, dtype=w.dtype)
    return (eye[:, None, :, None] * w[:, :, None, :]).reshape(nb * bi, nb * bj)


def kernel(x, positions, norm_mix, norm_mlp, w_mlp_in, w_mlp_out, w_in_even, w_out_even,
           hgrn_lb_logits, hgrn_out_norm, conv_w, conv_b, rg_wa, rg_ba, rg_wx, rg_bx, rg_lambda,
           w_in_odd, w_out_odd, q_norm, k_norm, sinks):
    bsz, seq, d = x.shape
    n = bsz * seq
    depth = norm_mix.shape[0]
    x2 = x.reshape(n, d)

    inv_freq = ROPE_THETA ** (-jnp.arange(0, C_HEAD_DIM, 2, dtype=F32) / C_HEAD_DIM)
    reps = LANES // (C_HEAD_DIM // 2)
    invf_tile = jnp.tile(inv_freq, reps).reshape(1, LANES)
    half_sign = jnp.concatenate([-jnp.ones((C_HEAD_DIM // 2,), F32), jnp.ones((C_HEAD_DIM // 2,), F32)])
    sign_tile = jnp.tile(half_sign, LANES // C_HEAD_DIM).reshape(1, LANES)
    cos_t, sin_t = _rope_tables(positions.reshape(n, 1), invf_tile, sign_tile)
    lane_head = jnp.arange(LANES) // C_HEAD_DIM
    seg_ones = (lane_head[:, None] == lane_head[None, :]).astype(BF16)

    w_mlp_in_b = w_mlp_in.astype(BF16)
    w_mlp_out_b = w_mlp_out.astype(BF16)
    w_in_even_b = w_in_even.astype(BF16)
    w_out_even_b = w_out_even.astype(BF16)
    w_in_odd_b = w_in_odd.astype(BF16)
    w_out_odd_b = w_out_odd.astype(BF16)

    for layer in range(depth):
        gain_mix = norm_mix[layer].reshape(1, d)
        gain_mlp = norm_mlp[layer].reshape(1, d)
        if layer % 2 == 0:
            e = layer // 2
            z = _inproj(x2, gain_mix, w_in_even_b[e])
            ya = _hgrn(z, hgrn_lb_logits, hgrn_out_norm[e].reshape(1, A_WIDTH), e, bsz, seq)
            w_gates = jnp.concatenate([_block_diag(rg_wa[e]), _block_diag(rg_wx[e])], axis=1).astype(BF16)
            b_gates = jnp.concatenate([rg_ba[e], rg_bx[e]]).reshape(1, 2 * B_WIDTH)
            yb = _rglru(z, conv_w[e], conv_b[e].reshape(1, B_WIDTH), w_gates, b_gates,
                        rg_lambda[e].reshape(1, B_WIDTH), bsz, seq)
            mixes, w_out = [ya, yb], w_out_even_b[e]
        else:
            o = layer // 2
            z = _inproj(x2, gain_mix, w_in_odd_b[o])
            q_gain = jnp.tile(q_norm[o], LANES // C_HEAD_DIM).reshape(1, LANES)
            k_gain = jnp.tile(k_norm[o], LANES // C_HEAD_DIM).reshape(1, LANES)
            attn = _attn(z, cos_t, sin_t, q_gain, k_gain, seg_ones, sinks[o], bsz, seq)
            mixes, w_out = [attn], w_out_odd_b[o]
        x2 = _mlp(x2, mixes, w_out, gain_mlp, w_mlp_in_b[layer], w_mlp_out_b[layer])
    return x2.reshape(bsz, seq, d)
```

```python
import functools

import numpy as np
import jax
import jax.numpy as jnp
from jax import lax
from jax.experimental import pallas as pl
from jax.experimental.pallas import tpu as pltpu

F32 = jnp.float32
BF16 = jnp.bfloat16

D_MODEL = 1024
D_FF = 4 * D_MODEL
RMS_EPS = 1e-6
NEG_BIG = -1e30
F_MIN = 1e-30
A_WIDTH = D_MODEL // 2
A_HEAD_DIM = 128
A_HEADS = A_WIDTH // A_HEAD_DIM
B_WIDTH = D_MODEL // 2
B_BLOCKS = 8
B_BLOCK_DIM = B_WIDTH // B_BLOCKS
B_CONV = 4
RG_C = 8.0
C_HEAD_DIM = 64
C_HEADS = D_MODEL // C_HEAD_DIM
C_KV_HEADS = 4
C_GROUP = C_HEADS // C_KV_HEADS
WINDOW = 128
ROPE_THETA = 10000.0
LOG2_E = 1.4426950408889634
EVEN_IN = 4 * A_WIDTH + 2 * B_WIDTH
ODD_IN = (C_HEADS + 2 * C_KV_HEADS) * C_HEAD_DIM

LANES = 128
SUBLANES = 8
VMEM_LIMIT_BYTES = 56 * 1024 * 1024

TM_PROJ = 512
TM_MLP = 512
FF_CHUNK = 1024
T_HGRN = 512
T_RGLRU = 256
T_ATTN = 512
T_ROPE = 1024


def _params(semantics):
    return pltpu.CompilerParams(dimension_semantics=semantics, vmem_limit_bytes=VMEM_LIMIT_BYTES)


def _rmsnorm(xf, gain):
    ms = jnp.mean(xf * xf, axis=-1, keepdims=True)
    return xf * lax.rsqrt(ms + RMS_EPS) * gain


def _sigmoid_tanh(x):
    return 0.5 * jnp.tanh(0.5 * x) + 0.5


def _inproj_kernel(x_ref, g_ref, w_ref, o_ref, *, chunk):
    h = _rmsnorm(x_ref[...], g_ref[...]).astype(BF16)
    n_out = o_ref.shape[1]
    for c in range(n_out // chunk):
        cols = slice(c * chunk, (c + 1) * chunk)
        o_ref[:, cols] = jnp.dot(h, w_ref[:, cols], preferred_element_type=F32)


def _inproj(x2, gain, w_bf16):
    n, d = x2.shape
    n_out = w_bf16.shape[1]
    chunk = 512
    return pl.pallas_call(
        functools.partial(_inproj_kernel, chunk=chunk),
        out_shape=jax.ShapeDtypeStruct((n, n_out), F32),
        grid=(n // TM_PROJ,),
        in_specs=[
            pl.BlockSpec((TM_PROJ, d), lambda i: (i, 0)),
            pl.BlockSpec((1, d), lambda i: (0, 0)),
            pl.BlockSpec((d, n_out), lambda i: (0, 0)),
        ],
        out_specs=pl.BlockSpec((TM_PROJ, n_out), lambda i: (i, 0)),
        compiler_params=_params(("parallel",)),
        name="inproj",
    )(x2, gain, w_bf16)


def _mlp_kernel(*refs, n_mix):
    x_ref = refs[0]
    mix_refs = refs[1:1 + n_mix]
    wo_refs = refs[1 + n_mix:1 + 2 * n_mix]
    g_ref, w1_ref, w2_ref, o_ref, act_ref = refs[1 + 2 * n_mix:]
    x1 = x_ref[...]
    for m_ref, wo_ref in zip(mix_refs, wo_refs):
        x1 = x1 + jnp.dot(m_ref[...], wo_ref[...], preferred_element_type=F32)
    o_ref[...] = x1
    h = _rmsnorm(x1, g_ref[...]).astype(BF16)
    for c in range(D_FF // FF_CHUNK):
        cols = slice(c * FF_CHUNK, (c + 1) * FF_CHUNK)
        u = jnp.maximum(jnp.dot(h, w1_ref[:, cols], preferred_element_type=F32), 0.0)
        act_ref[:, cols] = (u * u).astype(BF16)
    o_ref[...] += jnp.dot(act_ref[...], w2_ref[...], preferred_element_type=F32)


def _mlp(x2, mixes, w_out_bf16, gain, w1_bf16, w2_bf16):
    n, d = x2.shape
    n_mix = len(mixes)
    row = lambda i: (i, 0)
    const = lambda i: (0, 0)
    in_specs = [pl.BlockSpec((TM_MLP, d), row)]
    in_specs += [pl.BlockSpec((TM_MLP, m.shape[1]), row) for m in mixes]
    k = mixes[0].shape[1]
    in_specs += [pl.BlockSpec((k, d), functools.partial(lambda j, i: (j, 0), j)) for j in range(n_mix)]
    in_specs += [
        pl.BlockSpec((1, d), const),
        pl.BlockSpec((d, D_FF), const),
        pl.BlockSpec((D_FF, d), const),
    ]
    return pl.pallas_call(
        functools.partial(_mlp_kernel, n_mix=n_mix),
        out_shape=jax.ShapeDtypeStruct((n, d), F32),
        grid=(n // TM_MLP,),
        in_specs=in_specs,
        out_specs=pl.BlockSpec((TM_MLP, d), row),
        scratch_shapes=[pltpu.VMEM((TM_MLP, D_FF), BF16)],
        compiler_params=_params(("parallel",)),
        name="mlp",
    )(x2, *mixes, *([w_out_bf16] * n_mix), gain, w1_bf16, w2_bf16)


HGRN_CHUNK = 128
HGRN_LEVELS = (1, 2, 4, 8, 16, 32, 64)
HGRN_SMALL = tuple(m for m in HGRN_LEVELS if m < SUBLANES)


def _hgrn_constants():
    c = HGRN_CHUNK
    t = np.arange(c)[:, None]
    j = np.arange(c)[None, :]
    lower = (j <= t).astype(np.float32)
    blocks = [lower]
    for m in HGRN_SMALL:
        ref = (t // (2 * m)) * (2 * m) + m - 1
        blocks.append(lower - (j <= ref).astype(np.float32))
    cmat = np.concatenate(blocks, axis=0)
    cmat = np.concatenate([cmat, cmat], axis=1)
    masks = [np.eye(c, dtype=np.float32)]
    for m in HGRN_LEVELS:
        same_parent = (t // (2 * m)) == (j // (2 * m))
        masks.append((same_parent & (t % (2 * m) >= m) & (j % (2 * m) < m)).astype(np.float32))
    return cmat, np.stack(masks)


def _hgrn_kernel(q_ref, f_ref, v_ref, g_ref, lbl_ref, gn_ref, cmat_ref, mask_ref, o_ref,
                 st_ref, k_ref, gl_ref, oacc_ref, *, layer_e):
    t_rows = q_ref.shape[0]
    c = HGRN_CHUNK
    contract_lanes = (((1,), (1,)), ((), ()))

    @pl.when(pl.program_id(1) == 0)
    def _():
        st_ref[...] = jnp.zeros_like(st_ref)

    logits = lbl_ref[...]
    ex = jnp.exp(logits - jnp.max(logits, axis=0, keepdims=True))
    sm = ex / jnp.sum(ex, axis=0, keepdims=True)
    lb = jnp.sum(sm[:layer_e + 1], axis=0, keepdims=True) - sm[0:1]

    fx = f_ref[...]
    sig = 1.0 / (1.0 + jnp.exp(-fx))
    gl_ref[...] = jnp.log(jnp.maximum(lb + (1.0 - lb) * sig, F_MIN))
    k_ref[...] = (1.0 - lb) * (1.0 - sig)

    row8 = lax.broadcasted_iota(jnp.int32, (c, A_HEAD_DIM), 0) % SUBLANES

    def chunk(ci, carry):
        rows = pl.ds(pl.multiple_of(ci * c, c), c)
        g = gl_ref[rows, :]
        g_hi = g.astype(BF16)
        g_lo = (g - g_hi.astype(F32)).astype(BF16)
        dall = jnp.dot(cmat_ref[...], jnp.concatenate([g_hi, g_lo], axis=0),
                       preferred_element_type=F32)
        for h in range(A_HEADS):
            cols = slice(h * A_HEAD_DIM, (h + 1) * A_HEAD_DIM)
            q = q_ref[rows, cols]
            k = k_ref[rows, cols]
            v = v_ref[rows, cols]
            b = dall[0:c, cols]
            a = lax.dot_general(q.astype(BF16), k.astype(BF16), contract_lanes,
                                preferred_element_type=F32) * mask_ref[0]
            for li, m in enumerate(HGRN_LEVELS):
                if m < SUBLANES:
                    si = HGRN_SMALL.index(m) + 1
                    d = dall[si * c:(si + 1) * c, cols]
                    qk = jnp.where(row8 % (2 * m) >= m, q, k)
                else:
                    d_parts, qk_parts = [], []
                    for lo in range(0, c, 2 * m):
                        ref = lo + m - 1
                        d_parts.append(b[lo:lo + 2 * m] - b[ref:ref + 1])
                        qk_parts += [k[lo:lo + m], q[lo + m:lo + 2 * m]]
                    d = jnp.concatenate(d_parts, axis=0)
                    qk = jnp.concatenate(qk_parts, axis=0)
                x = (qk * jnp.exp(-jnp.abs(d))).astype(BF16)
                a = a + lax.dot_general(x, x, contract_lanes,
                                        preferred_element_type=F32) * mask_ref[li + 1]
            b_end = b[c - 1:c, :]
            q_dec = (q * jnp.exp(b)).astype(BF16)
            k_dec = (k * jnp.exp(b_end - b)).astype(BF16)
            v_t = v.T.astype(BF16)
            st = st_ref[h]
            lhs = jnp.concatenate([a.astype(BF16), q_dec], axis=1)
            rhs_t = jnp.concatenate([v_t, st.astype(BF16)], axis=1)
            oacc_ref[rows, cols] = lax.dot_general(lhs, rhs_t, contract_lanes,
                                                   preferred_element_type=F32)
            st_ref[h] = st * jnp.exp(b_end) + jnp.dot(v_t, k_dec, preferred_element_type=F32)
        return carry

    lax.fori_loop(0, t_rows // c, chunk, 0)

    gn = gn_ref[...]
    for h in range(A_HEADS):
        cols = slice(h * A_HEAD_DIM, (h + 1) * A_HEAD_DIM)
        o = oacc_ref[:, cols]
        on = o * lax.rsqrt(jnp.mean(o * o, axis=-1, keepdims=True) + RMS_EPS) * gn[:, cols]
        g = g_ref[:, cols]
        o_ref[:, cols] = (on * (g * _sigmoid_tanh(g))).astype(o_ref.dtype)


def _hgrn(z, lb_logits, out_norm, layer_e, bsz, seq):
    n = z.shape[0]
    tpb = seq // T_HGRN
    spec = lambda c: pl.BlockSpec((T_HGRN, A_WIDTH), lambda b, t: (b * tpb + t, c))
    const = lambda b, t: (0, 0)
    cmat, masks = _hgrn_constants()
    return pl.pallas_call(
        functools.partial(_hgrn_kernel, layer_e=layer_e),
        out_shape=jax.ShapeDtypeStruct((n, A_WIDTH), BF16),
        grid=(bsz, tpb),
        in_specs=[spec(0), spec(1), spec(2), spec(3),
                  pl.BlockSpec(lb_logits.shape, const),
                  pl.BlockSpec((1, A_WIDTH), const),
                  pl.BlockSpec(cmat.shape, const),
                  pl.BlockSpec(masks.shape, lambda b, t: (0, 0, 0))],
        out_specs=pl.BlockSpec((T_HGRN, A_WIDTH), lambda b, t: (b * tpb + t, 0)),
        scratch_shapes=[pltpu.VMEM((A_HEADS, A_HEAD_DIM, A_HEAD_DIM), F32),
                        pltpu.VMEM((T_HGRN, A_WIDTH), F32),
                        pltpu.VMEM((T_HGRN, A_WIDTH), F32),
                        pltpu.VMEM((T_HGRN, A_WIDTH), F32)],
        compiler_params=_params(("parallel", "arbitrary")),
        name="hgrn2",
    )(z, z, z, z, lb_logits, out_norm, jnp.asarray(cmat, BF16), jnp.asarray(masks, F32))


def _gelu_tanh(x):
    c = 0.7978845608028654
    return x * (0.5 + 0.5 * jnp.tanh(x * (c + (c * 0.044715) * (x * x))))


def _rglru_kernel(gate_ref, xb_ref, cw_ref, cb_ref, wg_ref, bg_ref, lam_ref, o_ref,
                  xext_ref, a_ref, u_ref, h_ref):
    t_rows = xb_ref.shape[0]
    pad = SUBLANES

    @pl.when(pl.program_id(1) == 0)
    def _():
        xext_ref[0:pad, :] = jnp.zeros((pad, B_WIDTH), F32)
        h_ref[...] = jnp.zeros_like(h_ref)

    xext_ref[pad:pad + t_rows, :] = xb_ref[...]
    cw = cw_ref[...]
    xc = cb_ref[...] + cw[B_CONV - 1:B_CONV, :] * xb_ref[...]
    for j in range(B_CONV - 1):
        shift = B_CONV - 1 - j
        xc = xc + cw[j:j + 1, :] * xext_ref[pad - shift:pad - shift + t_rows, :]
    xext_ref[0:pad, :] = xext_ref[t_rows:t_rows + pad, :]

    gz = jnp.dot(xc.astype(BF16), wg_ref[...], preferred_element_type=F32) + bg_ref[...]
    r = _sigmoid_tanh(gz[:, :B_WIDTH])
    i = _sigmoid_tanh(gz[:, B_WIDTH:])
    lam = lam_ref[...]
    softplus_neg_lam = jnp.maximum(-lam, 0.0) + jnp.log1p(jnp.exp(-jnp.abs(lam)))
    a = jnp.exp((-RG_C * softplus_neg_lam) * r)
    a_ref[...] = a
    y = jnp.maximum(1.0 - a * a, 0.0)
    u_ref[...] = (y * lax.rsqrt(jnp.maximum(y, F_MIN))) * (i * xc)

    row = lax.broadcasted_iota(jnp.int32, (SUBLANES, B_WIDTH), 0)

    def group(gidx, h_prev):
        rows = pl.ds(pl.multiple_of(gidx * SUBLANES, SUBLANES), SUBLANES)
        a_cum = a_ref[rows, :]
        h = u_ref[rows, :]
        for sh in (1, 2, 4):
            keep = row >= sh
            h = h + a_cum * jnp.where(keep, pltpu.roll(h, sh, axis=0), 0.0)
            a_cum = a_cum * jnp.where(keep, pltpu.roll(a_cum, sh, axis=0), 1.0)
        h = h + a_cum * h_prev
        u_ref[rows, :] = h
        return jnp.broadcast_to(h[SUBLANES - 1:SUBLANES, :], (SUBLANES, B_WIDTH))

    h_ref[...] = lax.fori_loop(0, t_rows // SUBLANES, group, h_ref[...])
    o_ref[...] = (u_ref[...] * _gelu_tanh(gate_ref[...])).astype(o_ref.dtype)


def _rglru(z, conv_w, conv_b, w_gates_bf16, b_gates, lam, bsz, seq):
    n = z.shape[0]
    tpb = seq // T_RGLRU
    spec = lambda c: pl.BlockSpec((T_RGLRU, B_WIDTH), lambda b, t: (b * tpb + t, c))
    const = lambda b, t: (0, 0)
    return pl.pallas_call(
        _rglru_kernel,
        out_shape=jax.ShapeDtypeStruct((n, B_WIDTH), BF16),
        grid=(bsz, tpb),
        in_specs=[spec(4), spec(5),
                  pl.BlockSpec((B_CONV, B_WIDTH), const),
                  pl.BlockSpec((1, B_WIDTH), const),
                  pl.BlockSpec((B_WIDTH, 2 * B_WIDTH), const),
                  pl.BlockSpec((1, 2 * B_WIDTH), const),
                  pl.BlockSpec((1, B_WIDTH), const)],
        out_specs=pl.BlockSpec((T_RGLRU, B_WIDTH), lambda b, t: (b * tpb + t, 0)),
        scratch_shapes=[pltpu.VMEM((T_RGLRU + 2 * SUBLANES, B_WIDTH), F32),
                        pltpu.VMEM((T_RGLRU, B_WIDTH), F32),
                        pltpu.VMEM((T_RGLRU, B_WIDTH), F32),
                        pltpu.VMEM((SUBLANES, B_WIDTH), F32)],
        compiler_params=_params(("parallel", "arbitrary")),
        name="rglru",
    )(z, z, conv_w, conv_b, w_gates_bf16, b_gates, lam)


def _rope_kernel(pos_ref, invf_ref, sign_ref, cos_ref, sin_ref):
    ang = pos_ref[...].astype(F32) * invf_ref[...]
    cos_ref[...] = jnp.cos(ang)
    sin_ref[...] = jnp.sin(ang) * sign_ref[...]


def _rope_tables(pos_col, invf_tile, sign_tile):
    n = pos_col.shape[0]
    const = lambda i: (0, 0)
    return pl.pallas_call(
        _rope_kernel,
        out_shape=(jax.ShapeDtypeStruct((n, LANES), F32), jax.ShapeDtypeStruct((n, LANES), F32)),
        grid=(n // T_ROPE,),
        in_specs=[pl.BlockSpec((T_ROPE, 1), lambda i: (i, 0)),
                  pl.BlockSpec((1, LANES), const),
                  pl.BlockSpec((1, LANES), const)],
        out_specs=(pl.BlockSpec((T_ROPE, LANES), lambda i: (i, 0)),
                   pl.BlockSpec((T_ROPE, LANES), lambda i: (i, 0))),
        compiler_params=_params(("parallel",)),
        name="rope_tables",
    )(pos_col, invf_tile, sign_tile)


def _attn_kernel(sinks_ref, q_ref, k_ref, v_ref, cos_ref, sin_ref, qg_ref, kg_ref, seg_ref, o_ref,
                 qlo_ref, qhi_ref, kd_ref, vd_ref):
    t_rows = q_ref.shape[0]
    n_blocks = t_rows // WINDOW
    first_tile = pl.program_id(1) == 0

    lane = lax.broadcasted_iota(jnp.int32, (t_rows, LANES), 1)
    low_half = lane < C_HEAD_DIM
    first_rot_half = (lane % C_HEAD_DIM) < (C_HEAD_DIM // 2)
    cos = cos_ref[...]
    sin = sin_ref[...]
    seg = seg_ref[...]

    def norm_rope(x, gain):
        ss = jnp.dot((x * x).astype(BF16), seg, preferred_element_type=F32)
        xn = x * lax.rsqrt(ss * (1.0 / C_HEAD_DIM) + RMS_EPS) * gain
        rot = jnp.where(first_rot_half, pltpu.roll(xn, LANES - C_HEAD_DIM // 2, axis=1),
                        pltpu.roll(xn, C_HEAD_DIM // 2, axis=1))
        return xn * cos + rot * sin

    @pl.when(first_tile)
    def _():
        kd_ref[:, 0:WINDOW, :] = jnp.zeros((C_KV_HEADS, WINDOW, LANES), BF16)
        vd_ref[:, 0:WINDOW, 0:LANES] = jnp.zeros((C_KV_HEADS, WINDOW, LANES), BF16)
        vd_ref[:, :, LANES:] = jnp.ones((C_KV_HEADS, t_rows + WINDOW, LANES), BF16)

    @pl.when(jnp.logical_not(first_tile))
    def _():
        kd_ref[:, 0:WINDOW, :] = kd_ref[:, t_rows:t_rows + WINDOW, :]
        vd_ref[:, 0:WINDOW, :] = vd_ref[:, t_rows:t_rows + WINDOW, :]

    for pair in range(C_KV_HEADS // 2):
        cols = slice(pair * LANES, (pair + 1) * LANES)
        kr = norm_rope(k_ref[:, cols], kg_ref[...])
        vr = v_ref[:, cols]
        k_sw = pltpu.roll(kr, C_HEAD_DIM, axis=1)
        v_sw = pltpu.roll(vr, C_HEAD_DIM, axis=1)
        kd_ref[2 * pair, WINDOW:, :] = jnp.where(low_half, kr, k_sw).astype(BF16)
        kd_ref[2 * pair + 1, WINDOW:, :] = jnp.where(low_half, k_sw, kr).astype(BF16)
        vd_ref[2 * pair, WINDOW:, 0:LANES] = jnp.where(low_half, vr, v_sw).astype(BF16)
        vd_ref[2 * pair + 1, WINDOW:, 0:LANES] = jnp.where(low_half, v_sw, vr).astype(BF16)

    scale = (C_HEAD_DIM ** -0.5) * LOG2_E
    for hp in range(C_HEADS // 2):
        cols = slice(hp * LANES, (hp + 1) * LANES)
        qr = norm_rope(q_ref[:, cols], qg_ref[...]) * scale
        qlo_ref[hp] = jnp.where(low_half, qr, 0.0).astype(BF16)
        qhi_ref[hp] = jnp.where(low_half, 0.0, qr).astype(BF16)

    qi = lax.broadcasted_iota(jnp.int32, (2 * WINDOW, 2 * WINDOW), 0) % WINDOW
    ki = lax.broadcasted_iota(jnp.int32, (2 * WINDOW, 2 * WINDOW), 1)
    rel = qi + WINDOW - ki
    in_band = (rel >= 0) & (rel < WINDOW)
    own = ki >= WINDOW
    upper_rows = lax.broadcasted_iota(jnp.int32, (2 * WINDOW, 1), 0) < WINDOW
    out_low = lax.broadcasted_iota(jnp.int32, (WINDOW, LANES), 1) < C_HEAD_DIM

    def qblock(j, carry):
        r0 = pl.multiple_of(j * WINDOW, WINDOW)
        has_prev = jnp.logical_or(j > 0, jnp.logical_not(first_tile))
        mask = in_band & (own | has_prev)
        for hp in range(C_HEADS // 2):
            kvh = (2 * hp) // C_GROUP
            lhs = jnp.concatenate([qlo_ref[hp, pl.ds(r0, WINDOW), :],
                                   qhi_ref[hp, pl.ds(r0, WINDOW), :]], axis=0)
            keys = kd_ref[kvh, pl.ds(r0, 2 * WINDOW), :]
            vals = vd_ref[kvh, pl.ds(r0, 2 * WINDOW), :]
            s = lax.dot_general(lhs, keys, (((1,), (1,)), ((), ())), preferred_element_type=F32)
            s = jnp.where(mask, s, NEG_BIG)
            sink = jnp.where(upper_rows, sinks_ref[2 * hp] * LOG2_E, sinks_ref[2 * hp + 1] * LOG2_E)
            m = jnp.maximum(jnp.max(s, axis=-1, keepdims=True), sink)
            p = jnp.exp2(s - m)
            pv_sum = jnp.dot(p.astype(BF16), vals, preferred_element_type=F32)
            pv = pv_sum[:, :LANES] / (pv_sum[:, LANES:] + jnp.exp2(sink - m))
            o_ref[pl.ds(r0, WINDOW), hp * LANES:(hp + 1) * LANES] = jnp.where(
                out_low, pv[:WINDOW], pv[WINDOW:]).astype(o_ref.dtype)
        return carry

    lax.fori_loop(0, n_blocks, qblock, 0)


def _attn(z, cos_t, sin_t, q_gain_tile, k_gain_tile, seg_ones, sinks, bsz, seq):
    n = z.shape[0]
    tpb = seq // T_ATTN
    rowmap = lambda c: (lambda b, t: (b * tpb + t, c))
    const = lambda b, t: (0, 0)
    kv_w = C_KV_HEADS * C_HEAD_DIM
    return pl.pallas_call(
        _attn_kernel,
        out_shape=jax.ShapeDtypeStruct((n, D_MODEL), BF16),
        grid=(bsz, tpb),
        in_specs=[pl.BlockSpec(memory_space=pltpu.SMEM),
                  pl.BlockSpec((T_ATTN, D_MODEL), rowmap(0)),
                  pl.BlockSpec((T_ATTN, kv_w), rowmap(D_MODEL // kv_w)),
                  pl.BlockSpec((T_ATTN, kv_w), rowmap(D_MODEL // kv_w + 1)),
                  pl.BlockSpec((T_ATTN, LANES), rowmap(0)),
                  pl.BlockSpec((T_ATTN, LANES), rowmap(0)),
                  pl.BlockSpec((1, LANES), const),
                  pl.BlockSpec((1, LANES), const),
                  pl.BlockSpec((LANES, LANES), const)],
        out_specs=pl.BlockSpec((T_ATTN, D_MODEL), rowmap(0)),
        scratch_shapes=[pltpu.VMEM((C_HEADS // 2, T_ATTN, LANES), BF16),
                        pltpu.VMEM((C_HEADS // 2, T_ATTN, LANES), BF16),
                        pltpu.VMEM((C_KV_HEADS, T_ATTN + WINDOW, LANES), BF16),
                        pltpu.VMEM((C_KV_HEADS, T_ATTN + WINDOW, 2 * LANES), BF16)],
        compiler_params=_params(("parallel", "arbitrary")),
        name="swa",
    )(sinks, z, z, z, cos_t, sin_t, q_gain_tile, k_gain_tile, seg_ones)


def _block_diag(w):
    nb, bi, bj = w.shape
    eye = jnp.eye(nb, dtype=w.dtype)
    return (eye[:, None, :, None] * w[:, :, None, :]).reshape(nb * bi, nb * bj)


def kernel(x, positions, norm_mix, norm_mlp, w_mlp_in, w_mlp_out, w_in_even, w_out_even,
           hgrn_lb_logits, hgrn_out_norm, conv_w, conv_b, rg_wa, rg_ba, rg_wx, rg_bx, rg_lambda,
           w_in_odd, w_out_odd, q_norm, k_norm, sinks):
    bsz, seq, d = x.shape
    n = bsz * seq
    depth = norm_mix.shape[0]
    x2 = x.reshape(n, d)

    inv_freq = ROPE_THETA ** (-jnp.arange(0, C_HEAD_DIM, 2, dtype=F32) / C_HEAD_DIM)
    reps = LANES // (C_HEAD_DIM // 2)
    invf_tile = jnp.tile(inv_freq, reps).reshape(1, LANES)
    half_sign = jnp.concatenate([-jnp.ones((C_HEAD_DIM // 2,), F32), jnp.ones((C_HEAD_DIM // 2,), F32)])
    sign_tile = jnp.tile(half_sign, LANES // C_HEAD_DIM).reshape(1, LANES)
    cos_t, sin_t = _rope_tables(positions.reshape(n, 1), invf_tile, sign_tile)
    lane_head = jnp.arange(LANES) // C_HEAD_DIM
    seg_ones = (lane_head[:, None] == lane_head[None, :]).astype(BF16)

    w_mlp_in_b = w_mlp_in.astype(BF16)
    w_mlp_out_b = w_mlp_out.astype(BF16)
    w_in_even_b = w_in_even.astype(BF16)
    w_out_even_b = w_out_even.astype(BF16)
    w_in_odd_b = w_in_odd.astype(BF16)
    w_out_odd_b = w_out_odd.astype(BF16)

    for layer in range(depth):
        gain_mix = norm_mix[layer].reshape(1, d)
        gain_mlp = norm_mlp[layer].reshape(1, d)
        if layer % 2 == 0:
            e = layer // 2
            z = _inproj(x2, gain_mix, w_in_even_b[e])
            ya = _hgrn(z, hgrn_lb_logits, hgrn_out_norm[e].reshape(1, A_WIDTH), e, bsz, seq)
            w_gates = jnp.concatenate([_block_diag(rg_wa[e]), _block_diag(rg_wx[e])], axis=1).astype(BF16)
            b_gates = jnp.concatenate([rg_ba[e], rg_bx[e]]).reshape(1, 2 * B_WIDTH)
            yb = _rglru(z, conv_w[e], conv_b[e].reshape(1, B_WIDTH), w_gates, b_gates,
                        rg_lambda[e].reshape(1, B_WIDTH), bsz, seq)
            mixes, w_out = [ya, yb], w_out_even_b[e]
        else:
            o = layer // 2
            z = _inproj(x2, gain_mix, w_in_odd_b[o])
            q_gain = jnp.tile(q_norm[o], LANES // C_HEAD_DIM).reshape(1, LANES)
            k_gain = jnp.tile(k_norm[o], LANES // C_HEAD_DIM).reshape(1, LANES)
            attn = _attn(z, cos_t, sin_t, q_gain, k_gain, seg_ones, sinks[o], bsz, seq)
            mixes, w_out = [attn], w_out_odd_b[o]
        x2 = _mlp(x2, mixes, w_out, gain_mlp, w_mlp_in_b[layer], w_mlp_out_b[layer])
    return x2.reshape(bsz, seq, d)
```

```python
import functools

import numpy as np
import jax
import jax.numpy as jnp
from jax import lax
from jax.experimental import pallas as pl
from jax.experimental.pallas import tpu as pltpu

F32 = jnp.float32
BF16 = jnp.bfloat16

D_MODEL = 1024
D_FF = 4 * D_MODEL
RMS_EPS = 1e-6
NEG_BIG = -1e30
F_MIN = 1e-30
A_WIDTH = D_MODEL // 2
A_HEAD_DIM = 128
A_HEADS = A_WIDTH // A_HEAD_DIM
B_WIDTH = D_MODEL // 2
B_BLOCKS = 8
B_BLOCK_DIM = B_WIDTH // B_BLOCKS
B_CONV = 4
RG_C = 8.0
C_HEAD_DIM = 64
C_HEADS = D_MODEL // C_HEAD_DIM
C_KV_HEADS = 4
C_GROUP = C_HEADS // C_KV_HEADS
WINDOW = 128
ROPE_THETA = 10000.0
LOG2_E = 1.4426950408889634
EVEN_IN = 4 * A_WIDTH + 2 * B_WIDTH
ODD_IN = (C_HEADS + 2 * C_KV_HEADS) * C_HEAD_DIM

LANES = 128
SUBLANES = 8
VMEM_LIMIT_BYTES = 56 * 1024 * 1024

TM_PROJ = 1024
T_HGRN = 512
T_TAIL = 512
T_ROPE = 1024
RGLRU_ROW_BLOCK = 128


def _params(semantics):
    return pltpu.CompilerParams(dimension_semantics=semantics, vmem_limit_bytes=VMEM_LIMIT_BYTES)


def _rmsnorm(xf, gain):
    ms = jnp.mean(xf * xf, axis=-1, keepdims=True)
    return xf * lax.rsqrt(ms + RMS_EPS) * gain


def _sigmoid_tanh(x):
    return 0.5 * jnp.tanh(0.5 * x) + 0.5


def _inproj_kernel(x_ref, g_ref, w_ref, o_ref, *, chunk):
    h = _rmsnorm(x_ref[...], g_ref[...]).astype(BF16)
    n_out = o_ref.shape[1]
    for c in range(n_out // chunk):
        cols = slice(c * chunk, (c + 1) * chunk)
        o_ref[:, cols] = jnp.dot(h, w_ref[:, cols], preferred_element_type=F32)


def _inproj(x2, gain, w_stack_bf16, layer):
    n, d = x2.shape
    n_out = w_stack_bf16.shape[2]
    chunk = 512
    return pl.pallas_call(
        functools.partial(_inproj_kernel, chunk=chunk),
        out_shape=jax.ShapeDtypeStruct((n, n_out), F32),
        grid=(n // TM_PROJ,),
        in_specs=[
            pl.BlockSpec((TM_PROJ, d), lambda i: (i, 0)),
            pl.BlockSpec((1, d), lambda i: (0, 0)),
            pl.BlockSpec((None, d, n_out), lambda i: (layer, 0, 0), pipeline_mode=pl.Buffered(1)),
        ],
        out_specs=pl.BlockSpec((TM_PROJ, n_out), lambda i: (i, 0)),
        compiler_params=_params(("parallel",)),
        name="inproj",
    )(x2, gain, w_stack_bf16)


def _residual_outproj(x_ref, mix_refs, wo_refs, o_ref):
    x1 = x_ref[...]
    for m_ref, wo_ref in zip(mix_refs, wo_refs):
        x1 = x1 + jnp.dot(m_ref[...], wo_ref[...], preferred_element_type=F32)
    o_ref[...] = x1


def _mlp_steps(g_ref, w1_ref, w2_ref, o_ref, act_ref):
    held = {}
    up_cols, down_cols = 512, 256

    def norm():
        held["h"] = _rmsnorm(o_ref[...], g_ref[...]).astype(BF16)

    def up(c):
        cols = slice(c * up_cols, (c + 1) * up_cols)
        u = jnp.maximum(jnp.dot(held["h"], w1_ref[:, cols], preferred_element_type=F32), 0.0)
        act_ref[:, cols] = (u * u).astype(BF16)

    def down(c):
        cols = slice(c * down_cols, (c + 1) * down_cols)
        o_ref[:, cols] += jnp.dot(act_ref[...], w2_ref[:, cols], preferred_element_type=F32)

    steps = [(0.2, norm)]
    steps += [(1.0, functools.partial(up, c)) for c in range(D_FF // up_cols)]
    steps += [(1.0, functools.partial(down, c)) for c in range(D_MODEL // down_cols)]
    return steps


def _run_interleaved(a_steps, b_steps):
    a_total = sum(c for c, _ in a_steps)
    b_total = sum(c for c, _ in b_steps)
    ia = ib = 0
    a_done = b_done = 0.0
    while ia < len(a_steps) or ib < len(b_steps):
        take_a = ib >= len(b_steps) or (ia < len(a_steps) and a_done / a_total <= b_done / b_total)
        if take_a:
            a_done += a_steps[ia][0]
            a_steps[ia][1]()
            ia += 1
        else:
            b_done += b_steps[ib][0]
            b_steps[ib][1]()
            ib += 1


HGRN_CHUNK = 128
HGRN_LEVELS = (1, 2, 4, 8, 16, 32, 64)
HGRN_SMALL = tuple(m for m in HGRN_LEVELS if m < SUBLANES)


def _hgrn_constants():
    c = HGRN_CHUNK
    t = np.arange(c)[:, None]
    j = np.arange(c)[None, :]
    lower = (j <= t).astype(np.float32)
    blocks = [lower]
    for m in HGRN_SMALL:
        ref = (t // (2 * m)) * (2 * m) + m - 1
        blocks.append(lower - (j <= ref).astype(np.float32))
    cmat = np.concatenate(blocks, axis=0)
    cmat = np.concatenate([cmat, cmat], axis=1)
    masks = [np.eye(c, dtype=np.float32)]
    for m in HGRN_LEVELS:
        same_parent = (t // (2 * m)) == (j // (2 * m))
        masks.append((same_parent & (t % (2 * m) >= m) & (j % (2 * m) < m)).astype(np.float32))
    return cmat, np.stack(masks)


def _hgrn_kernel(q_ref, f_ref, v_ref, g_ref, lbl_ref, gn_ref, cmat_ref, mask_ref, o_ref,
                 st_ref, k_ref, gl_ref, oacc_ref, *, layer_e):
    t_rows = q_ref.shape[0]
    c = HGRN_CHUNK
    contract_lanes = (((1,), (1,)), ((), ()))

    @pl.when(pl.program_id(1) == 0)
    def _():
        st_ref[...] = jnp.zeros_like(st_ref)

    logits = lbl_ref[...]
    ex = jnp.exp(logits - jnp.max(logits, axis=0, keepdims=True))
    sm = ex / jnp.sum(ex, axis=0, keepdims=True)
    lb = jnp.sum(sm[:layer_e + 1], axis=0, keepdims=True) - sm[0:1]

    fx = f_ref[...]
    sig = 1.0 / (1.0 + jnp.exp(-fx))
    gl_ref[...] = jnp.log(jnp.maximum(lb + (1.0 - lb) * sig, F_MIN))
    k_ref[...] = (1.0 - lb) * (1.0 - sig)

    row8 = lax.broadcasted_iota(jnp.int32, (c, A_HEAD_DIM), 0) % SUBLANES

    def chunk(ci, carry):
        rows = pl.ds(pl.multiple_of(ci * c, c), c)
        g = gl_ref[rows, :]
        g_hi = g.astype(BF16)
        g_lo = (g - g_hi.astype(F32)).astype(BF16)
        dall = jnp.dot(cmat_ref[...], jnp.concatenate([g_hi, g_lo], axis=0),
                       preferred_element_type=F32)
        for h in range(A_HEADS):
            cols = slice(h * A_HEAD_DIM, (h + 1) * A_HEAD_DIM)
            q = q_ref[rows, cols]
            k = k_ref[rows, cols]
            v = v_ref[rows, cols]
            b = dall[0:c, cols]
            level_x = []
            for li, m in enumerate(HGRN_LEVELS):
                if m < SUBLANES:
                    si = HGRN_SMALL.index(m) + 1
                    d = dall[si * c:(si + 1) * c, cols]
                    qk = jnp.where(row8 % (2 * m) >= m, q, k)
                else:
                    d_parts, qk_parts = [], []
                    for lo in range(0, c, 2 * m):
                        ref = lo + m - 1
                        d_parts.append(b[lo:lo + 2 * m] - b[ref:ref + 1])
                        qk_parts += [k[lo:lo + m], q[lo + m:lo + 2 * m]]
                    d = jnp.concatenate(d_parts, axis=0)
                    qk = jnp.concatenate(qk_parts, axis=0)
                level_x.append(qk * jnp.exp(-jnp.abs(d)))
            a = jnp.sum(q * k, axis=-1, keepdims=True) * mask_ref[0]
            for li, x in enumerate(level_x):
                a = a + jnp.dot(x.astype(BF16), x.T.astype(BF16),
                                preferred_element_type=F32) * mask_ref[li + 1]
            b_end = b[c - 1:c, :]
            q_dec = (q * jnp.exp(b)).astype(BF16)
            k_dec = (k * jnp.exp(b_end - b)).astype(BF16)
            v_t = v.T.astype(BF16)
            st = st_ref[h]
            lhs = jnp.concatenate([a.astype(BF16), q_dec], axis=1)
            rhs_t = jnp.concatenate([v_t, st.astype(BF16)], axis=1)
            oacc_ref[rows, cols] = lax.dot_general(lhs, rhs_t, contract_lanes,
                                                   preferred_element_type=F32)
            st_ref[h] = st * jnp.exp(b_end) + jnp.dot(v_t, k_dec, preferred_element_type=F32)
        return carry

    lax.fori_loop(0, t_rows // c, chunk, 0)

    gn = gn_ref[...]
    for h in range(A_HEADS):
        cols = slice(h * A_HEAD_DIM, (h + 1) * A_HEAD_DIM)
        o = oacc_ref[:, cols]
        on = o * lax.rsqrt(jnp.mean(o * o, axis=-1, keepdims=True) + RMS_EPS) * gn[:, cols]
        g = g_ref[:, cols]
        o_ref[:, cols] = (on * (g * _sigmoid_tanh(g))).astype(o_ref.dtype)


def _hgrn(z, lb_logits, out_norm, layer_e, bsz, seq):
    n = z.shape[0]
    tpb = seq // T_HGRN
    spec = lambda c: pl.BlockSpec((T_HGRN, A_WIDTH), lambda b, t: (b * tpb + t, c))
    const = lambda b, t: (0, 0)
    cmat, masks = _hgrn_constants()
    return pl.pallas_call(
        functools.partial(_hgrn_kernel, layer_e=layer_e),
        out_shape=jax.ShapeDtypeStruct((n, A_WIDTH), BF16),
        grid=(bsz, tpb),
        in_specs=[spec(0), spec(1), spec(2), spec(3),
                  pl.BlockSpec(lb_logits.shape, const),
                  pl.BlockSpec((1, A_WIDTH), const),
                  pl.BlockSpec(cmat.shape, const),
                  pl.BlockSpec(masks.shape, lambda b, t: (0, 0, 0))],
        out_specs=pl.BlockSpec((T_HGRN, A_WIDTH), lambda b, t: (b * tpb + t, 0)),
        scratch_shapes=[pltpu.VMEM((A_HEADS, A_HEAD_DIM, A_HEAD_DIM), F32),
                        pltpu.VMEM((T_HGRN, A_WIDTH), F32),
                        pltpu.VMEM((T_HGRN, A_WIDTH), F32),
                        pltpu.VMEM((T_HGRN, A_WIDTH), F32)],
        compiler_params=_params(("parallel", "arbitrary")),
        name="hgrn2",
    )(z, z, z, z, lb_logits, out_norm, jnp.asarray(cmat, BF16), jnp.asarray(masks, F32))


def _gelu_tanh(x):
    c = 0.7978845608028654
    return x * (0.5 + 0.5 * jnp.tanh(x * (c + (c * 0.044715) * (x * x))))


def _rglru_steps(gate_ref, xb_ref, cw_ref, cb_ref, wg_ref, bg_ref, lam_ref, o_ref,
                 xext_ref, a_ref, u_ref, h_ref, *, first_tile):
    t_rows = xb_ref.shape[0]
    pad = SUBLANES
    rb = RGLRU_ROW_BLOCK
    held = {}

    @pl.when(first_tile)
    def _():
        xext_ref[0:pad, :] = jnp.zeros((pad, B_WIDTH), F32)
        h_ref[...] = jnp.zeros_like(h_ref)

    lam = lam_ref[...]
    softplus_neg_lam = jnp.maximum(-lam, 0.0) + jnp.log1p(jnp.exp(-jnp.abs(lam)))
    log_a_scale = -RG_C * softplus_neg_lam
    cw = cw_ref[...]

    def gates(r0):
        rows = slice(r0, r0 + rb)
        xb = xb_ref[rows, :]
        xext_ref[pad + r0:pad + r0 + rb, :] = xb
        xc = cb_ref[...] + cw[B_CONV - 1:B_CONV, :] * xb
        for j in range(B_CONV - 1):
            shift = B_CONV - 1 - j
            xc = xc + cw[j:j + 1, :] * xext_ref[pad + r0 - shift:pad + r0 - shift + rb, :]
        if r0 + rb == t_rows:
            xext_ref[0:pad, :] = xext_ref[t_rows:t_rows + pad, :]
        gz = jnp.dot(xc.astype(BF16), wg_ref[...], preferred_element_type=F32) + bg_ref[...]
        r = _sigmoid_tanh(gz[:, :B_WIDTH])
        i = _sigmoid_tanh(gz[:, B_WIDTH:])
        a = jnp.exp(log_a_scale * r)
        a_ref[rows, :] = a
        y = jnp.maximum(1.0 - a * a, 0.0)
        u_ref[rows, :] = (y * lax.rsqrt(jnp.maximum(y, F_MIN))) * (i * xc)

    row = lax.broadcasted_iota(jnp.int32, (SUBLANES, B_WIDTH), 0)

    def scan(r0):
        h_prev = held.get("h")
        if h_prev is None:
            h_prev = h_ref[...]
        for g0 in range(r0, r0 + rb, SUBLANES):
            rows = slice(g0, g0 + SUBLANES)
            a_cum = a_ref[rows, :]
            h = u_ref[rows, :]
            for sh in (1, 2, 4):
                keep = row >= sh
                h = h + a_cum * jnp.where(keep, pltpu.roll(h, sh, axis=0), 0.0)
                a_cum = a_cum * jnp.where(keep, pltpu.roll(a_cum, sh, axis=0), 1.0)
            h = h + a_cum * h_prev
            u_ref[rows, :] = h
            h_prev = jnp.broadcast_to(h[SUBLANES - 1:SUBLANES, :], (SUBLANES, B_WIDTH))
        held["h"] = h_prev
        if r0 + rb == t_rows:
            h_ref[...] = h_prev

    def gate_out(r0):
        rows = slice(r0, r0 + rb)
        o_ref[rows, :] = (u_ref[rows, :] * _gelu_tanh(gate_ref[rows, :])).astype(o_ref.dtype)

    blocks = range(0, t_rows, rb)
    steps = [(3.0, functools.partial(gates, r0)) for r0 in blocks]
    steps += [(2.0, functools.partial(scan, r0)) for r0 in blocks]
    steps += [(1.0, functools.partial(gate_out, r0)) for r0 in blocks]
    return steps


def _rglru_mlp_kernel(gate_ref, xb_ref, cw_ref, cb_ref, wg_ref, bg_ref, lam_ref,
                      x_ref, ya_ref, woa_ref, wob_ref, g_ref, w1_ref, w2_ref, o_ref,
                      xext_ref, a_ref, u_ref, h_ref, yb_ref, act_ref, *, tiles_per_seq):
    i = pl.program_id(0)

    @pl.when(i == 0)
    def _():
        yb_ref[...] = jnp.zeros_like(yb_ref)

    _residual_outproj(x_ref, [ya_ref, yb_ref], [woa_ref, wob_ref], o_ref)
    rglru_steps = _rglru_steps(gate_ref, xb_ref, cw_ref, cb_ref, wg_ref, bg_ref, lam_ref, yb_ref,
                               xext_ref, a_ref, u_ref, h_ref, first_tile=i % tiles_per_seq == 0)
    _run_interleaved(rglru_steps, _mlp_steps(g_ref, w1_ref, w2_ref, o_ref, act_ref))


def _rglru_mlp(x2, z, ya, conv_w, conv_b, w_gates_bf16, b_gates, lam, w_out_stack, mix_layer,
               gain, w1_stack, w2_stack, layer, seq):
    n, d = x2.shape
    t = T_TAIL
    n_tiles = n // t
    cur = lambda c: (lambda i: (jnp.minimum(i, n_tiles - 1), c))
    prev = lambda i: (jnp.maximum(i - 1, 0), 0)
    const = lambda i: (0, 0)
    once = pl.Buffered(1)
    return pl.pallas_call(
        functools.partial(_rglru_mlp_kernel, tiles_per_seq=seq // t),
        out_shape=jax.ShapeDtypeStruct((n, d), F32),
        grid=(n_tiles + 1,),
        in_specs=[pl.BlockSpec((t, B_WIDTH), cur(4)),
                  pl.BlockSpec((t, B_WIDTH), cur(5)),
                  pl.BlockSpec((B_CONV, B_WIDTH), const),
                  pl.BlockSpec((1, B_WIDTH), const),
                  pl.BlockSpec((B_WIDTH, 2 * B_WIDTH), const),
                  pl.BlockSpec((1, 2 * B_WIDTH), const),
                  pl.BlockSpec((1, B_WIDTH), const),
                  pl.BlockSpec((t, d), prev),
                  pl.BlockSpec((t, A_WIDTH), prev),
                  pl.BlockSpec((None, A_WIDTH, d), lambda i: (mix_layer, 0, 0), pipeline_mode=once),
                  pl.BlockSpec((None, B_WIDTH, d), lambda i: (mix_layer, 1, 0), pipeline_mode=once),
                  pl.BlockSpec((1, d), const),
                  pl.BlockSpec((None, d, D_FF), lambda i: (layer, 0, 0), pipeline_mode=once),
                  pl.BlockSpec((None, D_FF, d), lambda i: (layer, 0, 0), pipeline_mode=once)],
        out_specs=pl.BlockSpec((t, d), prev),
        scratch_shapes=[pltpu.VMEM((t + 2 * SUBLANES, B_WIDTH), F32),
                        pltpu.VMEM((t, B_WIDTH), F32),
                        pltpu.VMEM((t, B_WIDTH), F32),
                        pltpu.VMEM((SUBLANES, B_WIDTH), F32),
                        pltpu.VMEM((t, B_WIDTH), BF16),
                        pltpu.VMEM((t, D_FF), BF16)],
        compiler_params=_params(("arbitrary",)),
        name="rglru_mlp",
    )(z, z, conv_w, conv_b, w_gates_bf16, b_gates, lam, x2, ya, w_out_stack, w_out_stack,
      gain, w1_stack, w2_stack)


def _rope_kernel(pos_ref, invf_ref, sign_ref, cos_ref, sin_ref):
    ang = pos_ref[...].astype(F32) * invf_ref[...]
    cos_ref[...] = jnp.cos(ang)
    sin_ref[...] = jnp.sin(ang) * sign_ref[...]


def _rope_tables(pos_col, invf_tile, sign_tile):
    n = pos_col.shape[0]
    const = lambda i: (0, 0)
    return pl.pallas_call(
        _rope_kernel,
        out_shape=(jax.ShapeDtypeStruct((n, LANES), F32), jax.ShapeDtypeStruct((n, LANES), F32)),
        grid=(n // T_ROPE,),
        in_specs=[pl.BlockSpec((T_ROPE, 1), lambda i: (i, 0)),
                  pl.BlockSpec((1, LANES), const),
                  pl.BlockSpec((1, LANES), const)],
        out_specs=(pl.BlockSpec((T_ROPE, LANES), lambda i: (i, 0)),
                   pl.BlockSpec((T_ROPE, LANES), lambda i: (i, 0))),
        compiler_params=_params(("parallel",)),
        name="rope_tables",
    )(pos_col, invf_tile, sign_tile)


def _attn_steps(sinks_ref, q_ref, k_ref, v_ref, cos_ref, sin_ref, qg_ref, kg_ref, seg_ref, o_ref,
                qlo_ref, qhi_ref, kd_ref, vd_ref, *, first_tile):
    t_rows = q_ref.shape[0]
    n_blocks = t_rows // WINDOW

    lane = lax.broadcasted_iota(jnp.int32, (t_rows, LANES), 1)
    low_half = lane < C_HEAD_DIM
    first_rot_half = (lane % C_HEAD_DIM) < (C_HEAD_DIM // 2)
    cos = cos_ref[...]
    sin = sin_ref[...]
    seg = seg_ref[...]

    def norm_rope(x, gain):
        ss = jnp.dot((x * x).astype(BF16), seg, preferred_element_type=F32)
        xn = x * lax.rsqrt(ss * (1.0 / C_HEAD_DIM) + RMS_EPS) * gain
        rot = jnp.where(first_rot_half, pltpu.roll(xn, LANES - C_HEAD_DIM // 2, axis=1),
                        pltpu.roll(xn, C_HEAD_DIM // 2, axis=1))
        return xn * cos + rot * sin

    @pl.when(first_tile)
    def _():
        kd_ref[:, 0:WINDOW, :] = jnp.zeros((C_KV_HEADS, WINDOW, LANES), BF16)
        vd_ref[:, 0:WINDOW, 0:LANES] = jnp.zeros((C_KV_HEADS, WINDOW, LANES), BF16)
        vd_ref[:, :, LANES:] = jnp.ones((C_KV_HEADS, t_rows + WINDOW, LANES), BF16)

    @pl.when(jnp.logical_not(first_tile))
    def _():
        kd_ref[:, 0:WINDOW, :] = kd_ref[:, t_rows:t_rows + WINDOW, :]
        vd_ref[:, 0:WINDOW, :] = vd_ref[:, t_rows:t_rows + WINDOW, :]

    def kv_prep(pair):
        cols = slice(pair * LANES, (pair + 1) * LANES)
        kr = norm_rope(k_ref[:, cols], kg_ref[...])
        vr = v_ref[:, cols]
        k_sw = pltpu.roll(kr, C_HEAD_DIM, axis=1)
        v_sw = pltpu.roll(vr, C_HEAD_DIM, axis=1)
        kd_ref[2 * pair, WINDOW:, :] = jnp.where(low_half, kr, k_sw).astype(BF16)
        kd_ref[2 * pair + 1, WINDOW:, :] = jnp.where(low_half, k_sw, kr).astype(BF16)
        vd_ref[2 * pair, WINDOW:, 0:LANES] = jnp.where(low_half, vr, v_sw).astype(BF16)
        vd_ref[2 * pair + 1, WINDOW:, 0:LANES] = jnp.where(low_half, v_sw, vr).astype(BF16)

    scale = (C_HEAD_DIM ** -0.5) * LOG2_E

    def q_prep(hp):
        cols = slice(hp * LANES, (hp + 1) * LANES)
        qr = norm_rope(q_ref[:, cols], qg_ref[...]) * scale
        qlo_ref[hp] = jnp.where(low_half, qr, 0.0).astype(BF16)
        qhi_ref[hp] = jnp.where(low_half, 0.0, qr).astype(BF16)

    qi = lax.broadcasted_iota(jnp.int32, (2 * WINDOW, 2 * WINDOW), 0) % WINDOW
    ki = lax.broadcasted_iota(jnp.int32, (2 * WINDOW, 2 * WINDOW), 1)
    rel = qi + WINDOW - ki
    in_band = (rel >= 0) & (rel < WINDOW)
    own = ki >= WINDOW
    upper_rows = lax.broadcasted_iota(jnp.int32, (2 * WINDOW, 1), 0) < WINDOW
    out_low = lax.broadcasted_iota(jnp.int32, (WINDOW, LANES), 1) < C_HEAD_DIM

    def head_pair(r0, hp, mask):
        kvh = (2 * hp) // C_GROUP
        lhs = jnp.concatenate([qlo_ref[hp, pl.ds(r0, WINDOW), :],
                               qhi_ref[hp, pl.ds(r0, WINDOW), :]], axis=0)
        keys = kd_ref[kvh, pl.ds(r0, 2 * WINDOW), :]
        vals = vd_ref[kvh, pl.ds(r0, 2 * WINDOW), :]
        s = lax.dot_general(lhs, keys, (((1,), (1,)), ((), ())), preferred_element_type=F32)
        s = jnp.where(mask, s, NEG_BIG)
        sink = jnp.where(upper_rows, sinks_ref[2 * hp] * LOG2_E, sinks_ref[2 * hp + 1] * LOG2_E)
        m = jnp.maximum(jnp.max(s, axis=-1, keepdims=True), sink)
        p = jnp.exp2(s - m)
        pv_sum = jnp.dot(p.astype(BF16), vals, preferred_element_type=F32)
        pv = pv_sum[:, :LANES] / (pv_sum[:, LANES:] + jnp.exp2(sink - m))
        o_ref[pl.ds(r0, WINDOW), hp * LANES:(hp + 1) * LANES] = jnp.where(
            out_low, pv[:WINDOW], pv[WINDOW:]).astype(o_ref.dtype)

    steps = [(2.0, functools.partial(kv_prep, pair)) for pair in range(C_KV_HEADS // 2)]
    steps += [(1.0, functools.partial(q_prep, hp)) for hp in range(C_HEADS // 2)]
    for j in range(n_blocks):
        mask = in_band if j > 0 else in_band & (own | jnp.logical_not(first_tile))
        steps += [(1.0, functools.partial(head_pair, j * WINDOW, hp, mask))
                  for hp in range(C_HEADS // 2)]
    return steps


def _attn_mlp_kernel(sinks_ref, q_ref, k_ref, v_ref, cos_ref, sin_ref, qg_ref, kg_ref, seg_ref,
                     x_ref, wo_ref, g_ref, w1_ref, w2_ref, o_ref,
                     qlo_ref, qhi_ref, kd_ref, vd_ref, attn_ref, act_ref, *, tiles_per_seq):
    i = pl.program_id(0)

    @pl.when(i == 0)
    def _():
        attn_ref[...] = jnp.zeros_like(attn_ref)

    _residual_outproj(x_ref, [attn_ref], [wo_ref], o_ref)
    attn_steps = _attn_steps(sinks_ref, q_ref, k_ref, v_ref, cos_ref, sin_ref, qg_ref, kg_ref, seg_ref,
                             attn_ref, qlo_ref, qhi_ref, kd_ref, vd_ref,
                             first_tile=i % tiles_per_seq == 0)
    _run_interleaved(attn_steps, _mlp_steps(g_ref, w1_ref, w2_ref, o_ref, act_ref))


def _attn_mlp(x2, z, cos_t, sin_t, q_gain_tile, k_gain_tile, seg_ones, sinks, w_out_stack, mix_layer,
              gain, w1_stack, w2_stack, layer, seq):
    n, d = x2.shape
    n_tiles = n // T_TAIL
    kv_w = C_KV_HEADS * C_HEAD_DIM
    cur = lambda c: (lambda i: (jnp.minimum(i, n_tiles - 1), c))
    prev = lambda i: (jnp.maximum(i - 1, 0), 0)
    const = lambda i: (0, 0)
    once = pl.Buffered(1)
    return pl.pallas_call(
        functools.partial(_attn_mlp_kernel, tiles_per_seq=seq // T_TAIL),
        out_shape=jax.ShapeDtypeStruct((n, d), F32),
        grid=(n_tiles + 1,),
        in_specs=[pl.BlockSpec(memory_space=pltpu.SMEM),
                  pl.BlockSpec((T_TAIL, D_MODEL), cur(0)),
                  pl.BlockSpec((T_TAIL, kv_w), cur(D_MODEL // kv_w)),
                  pl.BlockSpec((T_TAIL, kv_w), cur(D_MODEL // kv_w + 1)),
                  pl.BlockSpec((T_TAIL, LANES), cur(0)),
                  pl.BlockSpec((T_TAIL, LANES), cur(0)),
                  pl.BlockSpec((1, LANES), const),
                  pl.BlockSpec((1, LANES), const),
                  pl.BlockSpec((LANES, LANES), const),
                  pl.BlockSpec((T_TAIL, d), prev),
                  pl.BlockSpec((None, d, d), lambda i: (mix_layer, 0, 0), pipeline_mode=once),
                  pl.BlockSpec((1, d), const),
                  pl.BlockSpec((None, d, D_FF), lambda i: (layer, 0, 0), pipeline_mode=once),
                  pl.BlockSpec((None, D_FF, d), lambda i: (layer, 0, 0), pipeline_mode=once)],
        out_specs=pl.BlockSpec((T_TAIL, d), prev),
        scratch_shapes=[pltpu.VMEM((C_HEADS // 2, T_TAIL, LANES), BF16),
                        pltpu.VMEM((C_HEADS // 2, T_TAIL, LANES), BF16),
                        pltpu.VMEM((C_KV_HEADS, T_TAIL + WINDOW, LANES), BF16),
                        pltpu.VMEM((C_KV_HEADS, T_TAIL + WINDOW, 2 * LANES), BF16),
                        pltpu.VMEM((T_TAIL, d), BF16),
                        pltpu.VMEM((T_TAIL, D_FF), BF16)],
        compiler_params=_params(("arbitrary",)),
        name="swa_mlp",
    )(sinks, z, z, z, cos_t, sin_t, q_gain_tile, k_gain_tile, seg_ones,
      x2, w_out_stack, gain, w1_stack, w2_stack)


def _block_diag(w):
    nb, bi, bj = w.shape
    eye = jnp.eye(nb, dtype=w.dtype)
    return (eye[:, None, :, None] * w[:, :, None, :]).reshape(nb * bi, nb * bj)


def kernel(x, positions, norm_mix, norm_mlp, w_mlp_in, w_mlp_out, w_in_even, w_out_even,
           hgrn_lb_logits, hgrn_out_norm, conv_w, conv_b, rg_wa, rg_ba, rg_wx, rg_bx, rg_lambda,
           w_in_odd, w_out_odd, q_norm, k_norm, sinks):
    bsz, seq, d = x.shape
    n = bsz * seq
    depth = norm_mix.shape[0]
    x2 = x.reshape(n, d)

    inv_freq = ROPE_THETA ** (-jnp.arange(0, C_HEAD_DIM, 2, dtype=F32) / C_HEAD_DIM)
    reps = LANES // (C_HEAD_DIM // 2)
    invf_tile = jnp.tile(inv_freq, reps).reshape(1, LANES)
    half_sign = jnp.concatenate([-jnp.ones((C_HEAD_DIM // 2,), F32), jnp.ones((C_HEAD_DIM // 2,), F32)])
    sign_tile = jnp.tile(half_sign, LANES // C_HEAD_DIM).reshape(1, LANES)
    cos_t, sin_t = _rope_tables(positions.reshape(n, 1), invf_tile, sign_tile)
    lane_head = jnp.arange(LANES) // C_HEAD_DIM
    seg_ones = (lane_head[:, None] == lane_head[None, :]).astype(BF16)

    w_mlp_in_b = w_mlp_in.astype(BF16)
    w_mlp_out_b = w_mlp_out.astype(BF16)
    w_in_even_b = w_in_even.astype(BF16)
    w_out_even_b = w_out_even.astype(BF16)
    w_in_odd_b = w_in_odd.astype(BF16)
    w_out_odd_b = w_out_odd.astype(BF16)

    for layer in range(depth):
        gain_mix = norm_mix[layer].reshape(1, d)
        gain_mlp = norm_mlp[layer].reshape(1, d)
        if layer % 2 == 0:
            e = layer // 2
            z = _inproj(x2, gain_mix, w_in_even_b, e)
            ya = _hgrn(z, hgrn_lb_logits, hgrn_out_norm[e].reshape(1, A_WIDTH), e, bsz, seq)
            w_gates = jnp.concatenate([_block_diag(rg_wa[e]), _block_diag(rg_wx[e])], axis=1).astype(BF16)
            b_gates = jnp.concatenate([rg_ba[e], rg_bx[e]]).reshape(1, 2 * B_WIDTH)
            x2 = _rglru_mlp(x2, z, ya, conv_w[e], conv_b[e].reshape(1, B_WIDTH), w_gates, b_gates,
                            rg_lambda[e].reshape(1, B_WIDTH), w_out_even_b, e,
                            gain_mlp, w_mlp_in_b, w_mlp_out_b, layer, seq)
        else:
            o = layer // 2
            z = _inproj(x2, gain_mix, w_in_odd_b, o)
            q_gain = jnp.tile(q_norm[o], LANES // C_HEAD_DIM).reshape(1, LANES)
            k_gain = jnp.tile(k_norm[o], LANES // C_HEAD_DIM).reshape(1, LANES)
            x2 = _attn_mlp(x2, z, cos_t, sin_t, q_gain, k_gain, seg_ones, sinks[o], w_out_odd_b, o,
                           gain_mlp, w_mlp_in_b, w_mlp_out_b, layer, seq)
    return x2.reshape(bsz, seq, d)
```

```python
import functools

import numpy as np
import jax
import jax.numpy as jnp
from jax import lax
from jax.experimental import pallas as pl
from jax.experimental.pallas import tpu as pltpu

F32 = jnp.float32
BF16 = jnp.bfloat16

D_MODEL = 1024
D_FF = 4 * D_MODEL
RMS_EPS = 1e-6
NEG_BIG = -1e30
F_MIN = 1e-30
A_WIDTH = D_MODEL // 2
A_HEAD_DIM = 128
A_HEADS = A_WIDTH // A_HEAD_DIM
B_WIDTH = D_MODEL // 2
B_BLOCKS = 8
B_BLOCK_DIM = B_WIDTH // B_BLOCKS
B_CONV = 4
RG_C = 8.0
C_HEAD_DIM = 64
C_HEADS = D_MODEL // C_HEAD_DIM
C_KV_HEADS = 4
C_GROUP = C_HEADS // C_KV_HEADS
WINDOW = 128
ROPE_THETA = 10000.0
LOG2_E = 1.4426950408889634
EVEN_IN = 4 * A_WIDTH + 2 * B_WIDTH
ODD_IN = (C_HEADS + 2 * C_KV_HEADS) * C_HEAD_DIM

LANES = 128
SUBLANES = 8
VMEM_LIMIT_BYTES = 56 * 1024 * 1024

TM_PROJ = 1024
T_HGRN = 512
T_TAIL = 512
T_ROPE = 1024
RGLRU_ROW_BLOCK = 128


def _params(semantics):
    return pltpu.CompilerParams(dimension_semantics=semantics, vmem_limit_bytes=VMEM_LIMIT_BYTES)


def _rmsnorm(xf, gain):
    ms = jnp.mean(xf * xf, axis=-1, keepdims=True)
    return xf * lax.rsqrt(ms + RMS_EPS) * gain


def _sigmoid_tanh(x):
    return 0.5 * jnp.tanh(0.5 * x) + 0.5


def _inproj_kernel(x_ref, g_ref, w_ref, o_ref, *, chunk):
    h = _rmsnorm(x_ref[...], g_ref[...]).astype(BF16)
    n_out = o_ref.shape[1]
    for c in range(n_out // chunk):
        cols = slice(c * chunk, (c + 1) * chunk)
        o_ref[:, cols] = jnp.dot(h, w_ref[:, cols], preferred_element_type=F32)


def _inproj(x2, gain, w_stack_bf16, layer):
    n, d = x2.shape
    n_out = w_stack_bf16.shape[2]
    chunk = 512
    return pl.pallas_call(
        functools.partial(_inproj_kernel, chunk=chunk),
        out_shape=jax.ShapeDtypeStruct((n, n_out), F32),
        grid=(n // TM_PROJ,),
        in_specs=[
            pl.BlockSpec((TM_PROJ, d), lambda i: (i, 0)),
            pl.BlockSpec((1, d), lambda i: (0, 0)),
            pl.BlockSpec((None, d, n_out), lambda i: (layer, 0, 0), pipeline_mode=pl.Buffered(1)),
        ],
        out_specs=pl.BlockSpec((TM_PROJ, n_out), lambda i: (i, 0)),
        compiler_params=_params(("parallel",)),
        name="inproj",
    )(x2, gain, w_stack_bf16)


def _residual_outproj(x_ref, mix_refs, wo_refs, o_ref):
    x1 = x_ref[...]
    for m_ref, wo_ref in zip(mix_refs, wo_refs):
        x1 = x1 + jnp.dot(m_ref[...], wo_ref[...], preferred_element_type=F32)
    o_ref[...] = x1


def _mlp_steps(g_ref, w1_ref, w2_ref, o_ref, act_ref):
    held = {}
    up_cols, down_cols = 512, 256

    def norm():
        held["h"] = _rmsnorm(o_ref[...], g_ref[...]).astype(BF16)

    def up(c):
        cols = slice(c * up_cols, (c + 1) * up_cols)
        u = jnp.maximum(jnp.dot(held["h"], w1_ref[:, cols], preferred_element_type=F32), 0.0)
        act_ref[:, cols] = (u * u).astype(BF16)

    def down(c):
        cols = slice(c * down_cols, (c + 1) * down_cols)
        o_ref[:, cols] += jnp.dot(act_ref[...], w2_ref[:, cols], preferred_element_type=F32)

    steps = [(0.2, norm)]
    steps += [(1.0, functools.partial(up, c)) for c in range(D_FF // up_cols)]
    steps += [(1.0, functools.partial(down, c)) for c in range(D_MODEL // down_cols)]
    return steps


def _run_interleaved(a_steps, b_steps):
    a_total = sum(c for c, _ in a_steps)
    b_total = sum(c for c, _ in b_steps)
    ia = ib = 0
    a_done = b_done = 0.0
    while ia < len(a_steps) or ib < len(b_steps):
        take_a = ib >= len(b_steps) or (ia < len(a_steps) and a_done / a_total <= b_done / b_total)
        if take_a:
            a_done += a_steps[ia][0]
            a_steps[ia][1]()
            ia += 1
        else:
            b_done += b_steps[ib][0]
            b_steps[ib][1]()
            ib += 1


HGRN_CHUNK = 128
HGRN_LEVELS = (1, 2, 4, 8, 16, 32, 64)
HGRN_SMALL = tuple(m for m in HGRN_LEVELS if m < SUBLANES)


def _hgrn_constants():
    c = HGRN_CHUNK
    t = np.arange(c)[:, None]
    j = np.arange(c)[None, :]
    lower = (j <= t).astype(np.float32)
    blocks = [lower]
    for m in HGRN_SMALL:
        ref = (t // (2 * m)) * (2 * m) + m - 1
        blocks.append(lower - (j <= ref).astype(np.float32))
    cmat = np.concatenate(blocks, axis=0)
    cmat = np.concatenate([cmat, cmat], axis=1)
    masks = [np.eye(c, dtype=np.float32)]
    for m in HGRN_LEVELS:
        same_parent = (t // (2 * m)) == (j // (2 * m))
        masks.append((same_parent & (t % (2 * m) >= m) & (j % (2 * m) < m)).astype(np.float32))
    return cmat, np.stack(masks)


def _hgrn_kernel(q_ref, f_ref, v_ref, g_ref, lbl_ref, gn_ref, cmat_ref, mask_ref, o_ref,
                 st_ref, k_ref, gl_ref, oacc_ref, *, layer_e):
    t_rows = q_ref.shape[0]
    c = HGRN_CHUNK
    contract_lanes = (((1,), (1,)), ((), ()))

    @pl.when(pl.program_id(1) == 0)
    def _():
        st_ref[...] = jnp.zeros_like(st_ref)

    logits = lbl_ref[...]
    ex = jnp.exp(logits - jnp.max(logits, axis=0, keepdims=True))
    sm = ex / jnp.sum(ex, axis=0, keepdims=True)
    lb = jnp.sum(sm[:layer_e + 1], axis=0, keepdims=True) - sm[0:1]

    fx = f_ref[...]
    sig = 1.0 / (1.0 + jnp.exp(-fx))
    gl_ref[...] = jnp.log2(jnp.maximum(lb + (1.0 - lb) * sig, F_MIN))
    k_ref[...] = (1.0 - lb) * (1.0 - sig)

    def exp2_neg_abs(d):
        return jnp.exp2(-jnp.abs(d))

    row8 = lax.broadcasted_iota(jnp.int32, (c, A_HEAD_DIM), 0) % SUBLANES

    states = [st_ref[h] for h in range(A_HEADS)]

    def chunk(ci):
        rows = slice(ci * c, (ci + 1) * c)
        g = gl_ref[rows, :]
        g_hi = g.astype(BF16)
        g_lo = (g - g_hi.astype(F32)).astype(BF16)
        dall = jnp.dot(cmat_ref[...], jnp.concatenate([g_hi, g_lo], axis=0),
                       preferred_element_type=F32)
        for h in range(A_HEADS):
            cols = slice(h * A_HEAD_DIM, (h + 1) * A_HEAD_DIM)
            q = q_ref[rows, cols]
            k = k_ref[rows, cols]
            v = v_ref[rows, cols]
            b = dall[0:c, cols]
            level_x = []
            for li, m in enumerate(HGRN_LEVELS):
                if m < SUBLANES:
                    si = HGRN_SMALL.index(m) + 1
                    d = dall[si * c:(si + 1) * c, cols]
                    qk = jnp.where(row8 % (2 * m) >= m, q, k)
                else:
                    d_parts, qk_parts = [], []
                    for lo in range(0, c, 2 * m):
                        ref = lo + m - 1
                        d_parts.append(b[lo:lo + 2 * m] - b[ref:ref + 1])
                        qk_parts += [k[lo:lo + m], q[lo + m:lo + 2 * m]]
                    d = jnp.concatenate(d_parts, axis=0)
                    qk = jnp.concatenate(qk_parts, axis=0)
                level_x.append(qk * exp2_neg_abs(d))
            diag = jnp.sum(q * k, axis=-1, keepdims=True)
            a_rows = [diag[r0:r0 + SUBLANES] * mask_ref[0, r0:r0 + SUBLANES, :]
                      for r0 in range(0, c, SUBLANES)]
            for li, (m, x) in enumerate(zip(HGRN_LEVELS, level_x)):
                second = [r0 for r0 in range(0, c, SUBLANES) if m < SUBLANES or r0 % (2 * m) >= m]
                x_rows = x if len(second) == len(a_rows) else jnp.concatenate(
                    [x[r0:r0 + SUBLANES] for r0 in second], axis=0)
                gram = jnp.dot(x_rows.astype(BF16), x.T.astype(BF16), preferred_element_type=F32)
                for i, r0 in enumerate(second):
                    a_rows[r0 // SUBLANES] += (gram[i * SUBLANES:(i + 1) * SUBLANES]
                                               * mask_ref[li + 1, r0:r0 + SUBLANES, :])
            a = jnp.concatenate(a_rows, axis=0)
            b_end = b[c - 1:c, :]
            q_dec = (q * jnp.exp2(b)).astype(BF16)
            k_dec = (k * jnp.exp2(b_end - b)).astype(BF16)
            v_t = v.T.astype(BF16)
            st = states[h]
            lhs = jnp.concatenate([a.astype(BF16), q_dec], axis=1)
            rhs_t = jnp.concatenate([v_t, st.astype(BF16)], axis=1)
            oacc_ref[rows, cols] = lax.dot_general(lhs, rhs_t, contract_lanes,
                                                   preferred_element_type=F32)
            states[h] = st * jnp.exp2(b_end) + jnp.dot(v_t, k_dec, preferred_element_type=F32)

    for ci in range(t_rows // c):
        chunk(ci)
    for h in range(A_HEADS):
        st_ref[h] = states[h]

    gn = gn_ref[...]
    for h in range(A_HEADS):
        cols = slice(h * A_HEAD_DIM, (h + 1) * A_HEAD_DIM)
        o = oacc_ref[:, cols]
        on = o * lax.rsqrt(jnp.mean(o * o, axis=-1, keepdims=True) + RMS_EPS) * gn[:, cols]
        g = g_ref[:, cols]
        o_ref[:, cols] = (on * (g * _sigmoid_tanh(g))).astype(o_ref.dtype)


def _hgrn(z, lb_logits, out_norm, layer_e, bsz, seq):
    n = z.shape[0]
    tpb = seq // T_HGRN
    spec = lambda c: pl.BlockSpec((T_HGRN, A_WIDTH), lambda b, t: (b * tpb + t, c))
    const = lambda b, t: (0, 0)
    cmat, masks = _hgrn_constants()
    return pl.pallas_call(
        functools.partial(_hgrn_kernel, layer_e=layer_e),
        out_shape=jax.ShapeDtypeStruct((n, A_WIDTH), BF16),
        grid=(bsz, tpb),
        in_specs=[spec(0), spec(1), spec(2), spec(3),
                  pl.BlockSpec(lb_logits.shape, const),
                  pl.BlockSpec((1, A_WIDTH), const),
                  pl.BlockSpec(cmat.shape, const),
                  pl.BlockSpec(masks.shape, lambda b, t: (0, 0, 0))],
        out_specs=pl.BlockSpec((T_HGRN, A_WIDTH), lambda b, t: (b * tpb + t, 0)),
        scratch_shapes=[pltpu.VMEM((A_HEADS, A_HEAD_DIM, A_HEAD_DIM), F32),
                        pltpu.VMEM((T_HGRN, A_WIDTH), F32),
                        pltpu.VMEM((T_HGRN, A_WIDTH), F32),
                        pltpu.VMEM((T_HGRN, A_WIDTH), F32)],
        compiler_params=_params(("parallel", "arbitrary")),
        name="hgrn2",
    )(z, z, z, z, lb_logits, out_norm, jnp.asarray(cmat, BF16), jnp.asarray(masks, F32))


def _gelu_tanh(x):
    c = 0.7978845608028654
    return x * (0.5 + 0.5 * jnp.tanh(x * (c + (c * 0.044715) * (x * x))))


def _rglru_steps(gate_ref, xb_ref, cw_ref, cb_ref, wg_ref, bg_ref, lam_ref, o_ref,
                 xext_ref, a_ref, u_ref, h_ref, *, first_tile):
    t_rows = xb_ref.shape[0]
    pad = SUBLANES
    rb = RGLRU_ROW_BLOCK
    held = {}

    @pl.when(first_tile)
    def _():
        xext_ref[0:pad, :] = jnp.zeros((pad, B_WIDTH), F32)
        h_ref[...] = jnp.zeros_like(h_ref)

    lam = lam_ref[...]
    softplus_neg_lam = jnp.maximum(-lam, 0.0) + jnp.log1p(jnp.exp(-jnp.abs(lam)))
    log_a_scale = -RG_C * softplus_neg_lam
    cw = cw_ref[...]

    def gates(r0):
        rows = slice(r0, r0 + rb)
        xb = xb_ref[rows, :]
        xext_ref[pad + r0:pad + r0 + rb, :] = xb
        xc = cb_ref[...] + cw[B_CONV - 1:B_CONV, :] * xb
        for j in range(B_CONV - 1):
            shift = B_CONV - 1 - j
            xc = xc + cw[j:j + 1, :] * xext_ref[pad + r0 - shift:pad + r0 - shift + rb, :]
        if r0 + rb == t_rows:
            xext_ref[0:pad, :] = xext_ref[t_rows:t_rows + pad, :]
        gz = jnp.dot(xc.astype(BF16), wg_ref[...], preferred_element_type=F32) + bg_ref[...]
        r = _sigmoid_tanh(gz[:, :B_WIDTH])
        i = _sigmoid_tanh(gz[:, B_WIDTH:])
        a = jnp.exp(log_a_scale * r)
        a_ref[rows, :] = a
        y = jnp.maximum(1.0 - a * a, 0.0)
        u_ref[rows, :] = (y * lax.rsqrt(jnp.maximum(y, F_MIN))) * (i * xc)

    row = lax.broadcasted_iota(jnp.int32, (SUBLANES, B_WIDTH), 0)

    def scan(r0):
        h_prev = held.get("h")
        if h_prev is None:
            h_prev = h_ref[...]
        for g0 in range(r0, r0 + rb, SUBLANES):
            rows = slice(g0, g0 + SUBLANES)
            a_cum = a_ref[rows, :]
            h = u_ref[rows, :]
            for sh in (1, 2, 4):
                keep = row >= sh
                h = h + a_cum * jnp.where(keep, pltpu.roll(h, sh, axis=0), 0.0)
                a_cum = a_cum * jnp.where(keep, pltpu.roll(a_cum, sh, axis=0), 1.0)
            h = h + a_cum * h_prev
            u_ref[rows, :] = h
            h_prev = jnp.broadcast_to(h[SUBLANES - 1:SUBLANES, :], (SUBLANES, B_WIDTH))
        held["h"] = h_prev
        if r0 + rb == t_rows:
            h_ref[...] = h_prev

    def gate_out(r0):
        rows = slice(r0, r0 + rb)
        o_ref[rows, :] = (u_ref[rows, :] * _gelu_tanh(gate_ref[rows, :])).astype(o_ref.dtype)

    blocks = range(0, t_rows, rb)
    steps = [(3.0, functools.partial(gates, r0)) for r0 in blocks]
    steps += [(2.0, functools.partial(scan, r0)) for r0 in blocks]
    steps += [(1.0, functools.partial(gate_out, r0)) for r0 in blocks]
    return steps


def _rglru_mlp_kernel(gate_ref, xb_ref, cw_ref, cb_ref, wg_ref, bg_ref, lam_ref,
                      x_ref, ya_ref, woa_ref, wob_ref, g_ref, w1_ref, w2_ref, o_ref,
                      xext_ref, a_ref, u_ref, h_ref, yb_ref, act_ref, *, tiles_per_seq):
    i = pl.program_id(0)

    @pl.when(i == 0)
    def _():
        yb_ref[...] = jnp.zeros_like(yb_ref)

    _residual_outproj(x_ref, [ya_ref, yb_ref], [woa_ref, wob_ref], o_ref)
    rglru_steps = _rglru_steps(gate_ref, xb_ref, cw_ref, cb_ref, wg_ref, bg_ref, lam_ref, yb_ref,
                               xext_ref, a_ref, u_ref, h_ref, first_tile=i % tiles_per_seq == 0)
    _run_interleaved(rglru_steps, _mlp_steps(g_ref, w1_ref, w2_ref, o_ref, act_ref))


def _rglru_mlp(x2, z, ya, conv_w, conv_b, w_gates_bf16, b_gates, lam, w_out_stack, mix_layer,
               gain, w1_stack, w2_stack, layer, seq):
    n, d = x2.shape
    t = T_TAIL
    n_tiles = n // t
    cur = lambda c: (lambda i: (jnp.minimum(i, n_tiles - 1), c))
    prev = lambda i: (jnp.maximum(i - 1, 0), 0)
    const = lambda i: (0, 0)
    once = pl.Buffered(1)
    return pl.pallas_call(
        functools.partial(_rglru_mlp_kernel, tiles_per_seq=seq // t),
        out_shape=jax.ShapeDtypeStruct((n, d), F32),
        grid=(n_tiles + 1,),
        in_specs=[pl.BlockSpec((t, B_WIDTH), cur(4)),
                  pl.BlockSpec((t, B_WIDTH), cur(5)),
                  pl.BlockSpec((B_CONV, B_WIDTH), const),
                  pl.BlockSpec((1, B_WIDTH), const),
                  pl.BlockSpec((B_WIDTH, 2 * B_WIDTH), const),
                  pl.BlockSpec((1, 2 * B_WIDTH), const),
                  pl.BlockSpec((1, B_WIDTH), const),
                  pl.BlockSpec((t, d), prev),
                  pl.BlockSpec((t, A_WIDTH), prev),
                  pl.BlockSpec((None, A_WIDTH, d), lambda i: (mix_layer, 0, 0), pipeline_mode=once),
                  pl.BlockSpec((None, B_WIDTH, d), lambda i: (mix_layer, 1, 0), pipeline_mode=once),
                  pl.BlockSpec((1, d), const),
                  pl.BlockSpec((None, d, D_FF), lambda i: (layer, 0, 0), pipeline_mode=once),
                  pl.BlockSpec((None, D_FF, d), lambda i: (layer, 0, 0), pipeline_mode=once)],
        out_specs=pl.BlockSpec((t, d), prev),
        scratch_shapes=[pltpu.VMEM((t + 2 * SUBLANES, B_WIDTH), F32),
                        pltpu.VMEM((t, B_WIDTH), F32),
                        pltpu.VMEM((t, B_WIDTH), F32),
                        pltpu.VMEM((SUBLANES, B_WIDTH), F32),
                        pltpu.VMEM((t, B_WIDTH), BF16),
                        pltpu.VMEM((t, D_FF), BF16)],
        compiler_params=_params(("arbitrary",)),
        name="rglru_mlp",
    )(z, z, conv_w, conv_b, w_gates_bf16, b_gates, lam, x2, ya, w_out_stack, w_out_stack,
      gain, w1_stack, w2_stack)


def _rope_kernel(pos_ref, invf_ref, sign_ref, cos_ref, sin_ref):
    ang = pos_ref[...].astype(F32) * invf_ref[...]
    cos_ref[...] = jnp.cos(ang)
    sin_ref[...] = jnp.sin(ang) * sign_ref[...]


def _rope_tables(pos_col, invf_tile, sign_tile):
    n = pos_col.shape[0]
    const = lambda i: (0, 0)
    return pl.pallas_call(
        _rope_kernel,
        out_shape=(jax.ShapeDtypeStruct((n, LANES), F32), jax.ShapeDtypeStruct((n, LANES), F32)),
        grid=(n // T_ROPE,),
        in_specs=[pl.BlockSpec((T_ROPE, 1), lambda i: (i, 0)),
                  pl.BlockSpec((1, LANES), const),
                  pl.BlockSpec((1, LANES), const)],
        out_specs=(pl.BlockSpec((T_ROPE, LANES), lambda i: (i, 0)),
                   pl.BlockSpec((T_ROPE, LANES), lambda i: (i, 0))),
        compiler_params=_params(("parallel",)),
        name="rope_tables",
    )(pos_col, invf_tile, sign_tile)


def _attn_steps(sinks_ref, q_ref, k_ref, v_ref, cos_ref, sin_ref, qg_ref, kg_ref, seg_ref, o_ref,
                qlo_ref, qhi_ref, kd_ref, vd_ref, *, first_tile):
    t_rows = q_ref.shape[0]
    n_blocks = t_rows // WINDOW

    lane = lax.broadcasted_iota(jnp.int32, (t_rows, LANES), 1)
    low_half = lane < C_HEAD_DIM
    first_rot_half = (lane % C_HEAD_DIM) < (C_HEAD_DIM // 2)
    cos = cos_ref[...]
    sin = sin_ref[...]
    seg = seg_ref[...]

    def norm_rope(x, gain):
        ss = jnp.dot((x * x).astype(BF16), seg, preferred_element_type=F32)
        xn = x * lax.rsqrt(ss * (1.0 / C_HEAD_DIM) + RMS_EPS) * gain
        rot = jnp.where(first_rot_half, pltpu.roll(xn, LANES - C_HEAD_DIM // 2, axis=1),
                        pltpu.roll(xn, C_HEAD_DIM // 2, axis=1))
        return xn * cos + rot * sin

    @pl.when(first_tile)
    def _():
        kd_ref[:, 0:WINDOW, :] = jnp.zeros((C_KV_HEADS, WINDOW, LANES), BF16)
        vd_ref[:, 0:WINDOW, 0:LANES] = jnp.zeros((C_KV_HEADS, WINDOW, LANES), BF16)
        vd_ref[:, :, LANES:] = jnp.ones((C_KV_HEADS, t_rows + WINDOW, LANES), BF16)

    @pl.when(jnp.logical_not(first_tile))
    def _():
        kd_ref[:, 0:WINDOW, :] = kd_ref[:, t_rows:t_rows + WINDOW, :]
        vd_ref[:, 0:WINDOW, :] = vd_ref[:, t_rows:t_rows + WINDOW, :]

    def kv_prep(pair):
        cols = slice(pair * LANES, (pair + 1) * LANES)
        kr = norm_rope(k_ref[:, cols], kg_ref[...])
        vr = v_ref[:, cols]
        k_sw = pltpu.roll(kr, C_HEAD_DIM, axis=1)
        v_sw = pltpu.roll(vr, C_HEAD_DIM, axis=1)
        kd_ref[2 * pair, WINDOW:, :] = jnp.where(low_half, kr, k_sw).astype(BF16)
        kd_ref[2 * pair + 1, WINDOW:, :] = jnp.where(low_half, k_sw, kr).astype(BF16)
        vd_ref[2 * pair, WINDOW:, 0:LANES] = jnp.where(low_half, vr, v_sw).astype(BF16)
        vd_ref[2 * pair + 1, WINDOW:, 0:LANES] = jnp.where(low_half, v_sw, vr).astype(BF16)

    scale = (C_HEAD_DIM ** -0.5) * LOG2_E

    def q_prep(hp):
        cols = slice(hp * LANES, (hp + 1) * LANES)
        qr = norm_rope(q_ref[:, cols], qg_ref[...]) * scale
        qlo_ref[hp] = jnp.where(low_half, qr, 0.0).astype(BF16)
        qhi_ref[hp] = jnp.where(low_half, 0.0, qr).astype(BF16)

    qi = lax.broadcasted_iota(jnp.int32, (2 * WINDOW, 2 * WINDOW), 0) % WINDOW
    ki = lax.broadcasted_iota(jnp.int32, (2 * WINDOW, 2 * WINDOW), 1)
    rel = qi + WINDOW - ki
    in_band = (rel >= 0) & (rel < WINDOW)
    own = ki >= WINDOW
    upper_rows = lax.broadcasted_iota(jnp.int32, (2 * WINDOW, 1), 0) < WINDOW
    out_low = lax.broadcasted_iota(jnp.int32, (WINDOW, LANES), 1) < C_HEAD_DIM

    def head_pair(r0, hp, mask):
        kvh = (2 * hp) // C_GROUP
        lhs = jnp.concatenate([qlo_ref[hp, pl.ds(r0, WINDOW), :],
                               qhi_ref[hp, pl.ds(r0, WINDOW), :]], axis=0)
        keys = kd_ref[kvh, pl.ds(r0, 2 * WINDOW), :]
        vals = vd_ref[kvh, pl.ds(r0, 2 * WINDOW), :]
        s = lax.dot_general(lhs, keys, (((1,), (1,)), ((), ())), preferred_element_type=F32)
        s = jnp.where(mask, s, NEG_BIG)
        sink = jnp.where(upper_rows, sinks_ref[2 * hp] * LOG2_E, sinks_ref[2 * hp + 1] * LOG2_E)
        m = jnp.maximum(jnp.max(s, axis=-1, keepdims=True), sink)
        p = jnp.exp2(s - m)
        pv_sum = jnp.dot(p.astype(BF16), vals, preferred_element_type=F32)
        pv = pv_sum[:, :LANES] / (pv_sum[:, LANES:] + jnp.exp2(sink - m))
        o_ref[pl.ds(r0, WINDOW), hp * LANES:(hp + 1) * LANES] = jnp.where(
            out_low, pv[:WINDOW], pv[WINDOW:]).astype(o_ref.dtype)

    steps = [(2.0, functools.partial(kv_prep, pair)) for pair in range(C_KV_HEADS // 2)]
    steps += [(1.0, functools.partial(q_prep, hp)) for hp in range(C_HEADS // 2)]
    for j in range(n_blocks):
        mask = in_band if j > 0 else in_band & (own | jnp.logical_not(first_tile))
        steps += [(1.0, functools.partial(head_pair, j * WINDOW, hp, mask))
                  for hp in range(C_HEADS // 2)]
    return steps


def _attn_mlp_kernel(sinks_ref, q_ref, k_ref, v_ref, cos_ref, sin_ref, qg_ref, kg_ref, seg_ref,
                     x_ref, wo_ref, g_ref, w1_ref, w2_ref, o_ref,
                     qlo_ref, qhi_ref, kd_ref, vd_ref, attn_ref, act_ref, *, tiles_per_seq):
    i = pl.program_id(0)

    @pl.when(i == 0)
    def _():
        attn_ref[...] = jnp.zeros_like(attn_ref)

    _residual_outproj(x_ref, [attn_ref], [wo_ref], o_ref)
    attn_steps = _attn_steps(sinks_ref, q_ref, k_ref, v_ref, cos_ref, sin_ref, qg_ref, kg_ref, seg_ref,
                             attn_ref, qlo_ref, qhi_ref, kd_ref, vd_ref,
                             first_tile=i % tiles_per_seq == 0)
    _run_interleaved(attn_steps, _mlp_steps(g_ref, w1_ref, w2_ref, o_ref, act_ref))


def _attn_mlp(x2, z, cos_t, sin_t, q_gain_tile, k_gain_tile, seg_ones, sinks, w_out_stack, mix_layer,
              gain, w1_stack, w2_stack, layer, seq):
    n, d = x2.shape
    n_tiles = n // T_TAIL
    kv_w = C_KV_HEADS * C_HEAD_DIM
    cur = lambda c: (lambda i: (jnp.minimum(i, n_tiles - 1), c))
    prev = lambda i: (jnp.maximum(i - 1, 0), 0)
    const = lambda i: (0, 0)
    once = pl.Buffered(1)
    return pl.pallas_call(
        functools.partial(_attn_mlp_kernel, tiles_per_seq=seq // T_TAIL),
        out_shape=jax.ShapeDtypeStruct((n, d), F32),
        grid=(n_tiles + 1,),
        in_specs=[pl.BlockSpec(memory_space=pltpu.SMEM),
                  pl.BlockSpec((T_TAIL, D_MODEL), cur(0)),
                  pl.BlockSpec((T_TAIL, kv_w), cur(D_MODEL // kv_w)),
                  pl.BlockSpec((T_TAIL, kv_w), cur(D_MODEL // kv_w + 1)),
                  pl.BlockSpec((T_TAIL, LANES), cur(0)),
                  pl.BlockSpec((T_TAIL, LANES), cur(0)),
                  pl.BlockSpec((1, LANES), const),
                  pl.BlockSpec((1, LANES), const),
                  pl.BlockSpec((LANES, LANES), const),
                  pl.BlockSpec((T_TAIL, d), prev),
                  pl.BlockSpec((None, d, d), lambda i: (mix_layer, 0, 0), pipeline_mode=once),
                  pl.BlockSpec((1, d), const),
                  pl.BlockSpec((None, d, D_FF), lambda i: (layer, 0, 0), pipeline_mode=once),
                  pl.BlockSpec((None, D_FF, d), lambda i: (layer, 0, 0), pipeline_mode=once)],
        out_specs=pl.BlockSpec((T_TAIL, d), prev),
        scratch_shapes=[pltpu.VMEM((C_HEADS // 2, T_TAIL, LANES), BF16),
                        pltpu.VMEM((C_HEADS // 2, T_TAIL, LANES), BF16),
                        pltpu.VMEM((C_KV_HEADS, T_TAIL + WINDOW, LANES), BF16),
                        pltpu.VMEM((C_KV_HEADS, T_TAIL + WINDOW, 2 * LANES), BF16),
                        pltpu.VMEM((T_TAIL, d), BF16),
                        pltpu.VMEM((T_TAIL, D_FF), BF16)],
        compiler_params=_params(("arbitrary",)),
        name="swa_mlp",
    )(sinks, z, z, z, cos_t, sin_t, q_gain_tile, k_gain_tile, seg_ones,
      x2, w_out_stack, gain, w1_stack, w2_stack)


def _block_diag(w):
    nb, bi, bj = w.shape
    eye = jnp.eye(nb, dtype=w.dtype)
    return (eye[:, None, :, None] * w[:, :, None, :]).reshape(nb * bi, nb * bj)


def kernel(x, positions, norm_mix, norm_mlp, w_mlp_in, w_mlp_out, w_in_even, w_out_even,
           hgrn_lb_logits, hgrn_out_norm, conv_w, conv_b, rg_wa, rg_ba, rg_wx, rg_bx, rg_lambda,
           w_in_odd, w_out_odd, q_norm, k_norm, sinks):
    bsz, seq, d = x.shape
    n = bsz * seq
    depth = norm_mix.shape[0]
    x2 = x.reshape(n, d)

    inv_freq = ROPE_THETA ** (-jnp.arange(0, C_HEAD_DIM, 2, dtype=F32) / C_HEAD_DIM)
    reps = LANES // (C_HEAD_DIM // 2)
    invf_tile = jnp.tile(inv_freq, reps).reshape(1, LANES)
    half_sign = jnp.concatenate([-jnp.ones((C_HEAD_DIM // 2,), F32), jnp.ones((C_HEAD_DIM // 2,), F32)])
    sign_tile = jnp.tile(half_sign, LANES // C_HEAD_DIM).reshape(1, LANES)
    cos_t, sin_t = _rope_tables(positions.reshape(n, 1), invf_tile, sign_tile)
    lane_head = jnp.arange(LANES) // C_HEAD_DIM
    seg_ones = (lane_head[:, None] == lane_head[None, :]).astype(BF16)

    w_mlp_in_b = w_mlp_in.astype(BF16)
    w_mlp_out_b = w_mlp_out.astype(BF16)
    w_in_even_b = w_in_even.astype(BF16)
    w_out_even_b = w_out_even.astype(BF16)
    w_in_odd_b = w_in_odd.astype(BF16)
    w_out_odd_b = w_out_odd.astype(BF16)

    for layer in range(depth):
        gain_mix = norm_mix[layer].reshape(1, d)
        gain_mlp = norm_mlp[layer].reshape(1, d)
        if layer % 2 == 0:
            e = layer // 2
            z = _inproj(x2, gain_mix, w_in_even_b, e)
            ya = _hgrn(z, hgrn_lb_logits, hgrn_out_norm[e].reshape(1, A_WIDTH), e, bsz, seq)
            w_gates = jnp.concatenate([_block_diag(rg_wa[e]), _block_diag(rg_wx[e])], axis=1).astype(BF16)
            b_gates = jnp.concatenate([rg_ba[e], rg_bx[e]]).reshape(1, 2 * B_WIDTH)
            x2 = _rglru_mlp(x2, z, ya, conv_w[e], conv_b[e].reshape(1, B_WIDTH), w_gates, b_gates,
                            rg_lambda[e].reshape(1, B_WIDTH), w_out_even_b, e,
                            gain_mlp, w_mlp_in_b, w_mlp_out_b, layer, seq)
        else:
            o = layer // 2
            z = _inproj(x2, gain_mix, w_in_odd_b, o)
            q_gain = jnp.tile(q_norm[o], LANES // C_HEAD_DIM).reshape(1, LANES)
            k_gain = jnp.tile(k_norm[o], LANES // C_HEAD_DIM).reshape(1, LANES)
            x2 = _attn_mlp(x2, z, cos_t, sin_t, q_gain, k_gain, seg_ones, sinks[o], w_out_odd_b, o,
                           gain_mlp, w_mlp_in_b, w_mlp_out_b, layer, seq)
    return x2.reshape(bsz, seq, d)
```

```python
import functools

import numpy as np
import jax
import jax.numpy as jnp
from jax import lax
from jax.experimental import pallas as pl
from jax.experimental.pallas import tpu as pltpu

F32 = jnp.float32
BF16 = jnp.bfloat16

D_MODEL = 1024
D_FF = 4 * D_MODEL
RMS_EPS = 1e-6
NEG_BIG = -1e30
F_MIN = 1e-30
A_WIDTH = D_MODEL // 2
A_HEAD_DIM = 128
A_HEADS = A_WIDTH // A_HEAD_DIM
B_WIDTH = D_MODEL // 2
B_BLOCKS = 8
B_BLOCK_DIM = B_WIDTH // B_BLOCKS
B_CONV = 4
RG_C = 8.0
C_HEAD_DIM = 64
C_HEADS = D_MODEL // C_HEAD_DIM
C_KV_HEADS = 4
C_GROUP = C_HEADS // C_KV_HEADS
WINDOW = 128
ROPE_THETA = 10000.0
LOG2_E = 1.4426950408889634
EVEN_IN = 4 * A_WIDTH + 2 * B_WIDTH
ODD_IN = (C_HEADS + 2 * C_KV_HEADS) * C_HEAD_DIM

LANES = 128
SUBLANES = 8
VMEM_LIMIT_BYTES = 56 * 1024 * 1024

TM_PROJ = 1024
T_HGRN = 512
T_TAIL = 512
T_ROPE = 1024
RGLRU_ROW_BLOCK = 128
GATE_SLAB = 256


def _params(semantics):
    return pltpu.CompilerParams(dimension_semantics=semantics, vmem_limit_bytes=VMEM_LIMIT_BYTES)


def _rmsnorm(xf, gain):
    ms = jnp.mean(xf * xf, axis=-1, keepdims=True)
    return xf * lax.rsqrt(ms + RMS_EPS) * gain


def _sigmoid_tanh(x):
    return 0.5 * jnp.tanh(0.5 * x) + 0.5


def _inproj_kernel(x_ref, g_ref, w_ref, o_ref, *, chunk):
    h = _rmsnorm(x_ref[...], g_ref[...]).astype(BF16)
    n_out = o_ref.shape[1]
    for c in range(n_out // chunk):
        cols = slice(c * chunk, (c + 1) * chunk)
        o_ref[:, cols] = jnp.dot(h, w_ref[:, cols], preferred_element_type=F32).astype(o_ref.dtype)


def _inproj(x2, gain, w_stack_bf16, layer, out_dtype):
    n, d = x2.shape
    n_out = w_stack_bf16.shape[2]
    chunk = 512
    return pl.pallas_call(
        functools.partial(_inproj_kernel, chunk=chunk),
        out_shape=jax.ShapeDtypeStruct((n, n_out), out_dtype),
        grid=(n // TM_PROJ,),
        in_specs=[
            pl.BlockSpec((TM_PROJ, d), lambda i: (i, 0)),
            pl.BlockSpec((1, d), lambda i: (0, 0)),
            pl.BlockSpec((None, d, n_out), lambda i: (layer, 0, 0), pipeline_mode=pl.Buffered(1)),
        ],
        out_specs=pl.BlockSpec((TM_PROJ, n_out), lambda i: (i, 0)),
        compiler_params=_params(("parallel",)),
        name="inproj",
    )(x2, gain, w_stack_bf16)


def _residual_outproj(x_ref, mix_refs, wo_refs, o_ref):
    x1 = x_ref[...]
    for m_ref, wo_ref in zip(mix_refs, wo_refs):
        x1 = x1 + jnp.dot(m_ref[...], wo_ref[...], preferred_element_type=F32)
    o_ref[...] = x1


def _mlp_steps(g_ref, w1_ref, w2_ref, o_ref, act_ref):
    held = {}
    up_cols, down_cols = 512, 256

    def norm():
        held["h"] = _rmsnorm(o_ref[...], g_ref[...]).astype(BF16)

    def up(c):
        cols = slice(c * up_cols, (c + 1) * up_cols)
        u = jnp.maximum(jnp.dot(held["h"], w1_ref[:, cols], preferred_element_type=F32), 0.0)
        act_ref[:, cols] = (u * u).astype(BF16)

    def down(c):
        cols = slice(c * down_cols, (c + 1) * down_cols)
        o_ref[:, cols] += jnp.dot(act_ref[...], w2_ref[:, cols], preferred_element_type=F32)

    steps = [(0.2, norm)]
    steps += [(1.0, functools.partial(up, c)) for c in range(D_FF // up_cols)]
    steps += [(1.0, functools.partial(down, c)) for c in range(D_MODEL // down_cols)]
    return steps


def _run_interleaved(a_steps, b_steps):
    a_total = sum(c for c, _ in a_steps)
    b_total = sum(c for c, _ in b_steps)
    ia = ib = 0
    a_done = b_done = 0.0
    while ia < len(a_steps) or ib < len(b_steps):
        take_a = ib >= len(b_steps) or (ia < len(a_steps) and a_done / a_total <= b_done / b_total)
        if take_a:
            a_done += a_steps[ia][0]
            a_steps[ia][1]()
            ia += 1
        else:
            b_done += b_steps[ib][0]
            b_steps[ib][1]()
            ib += 1


HGRN_CHUNK = 128
HGRN_LEVELS = (1, 2, 4, 8, 16, 32, 64)
HGRN_SMALL = tuple(m for m in HGRN_LEVELS if m < SUBLANES)


def _hgrn_constants():
    c = HGRN_CHUNK
    t = np.arange(c)[:, None]
    j = np.arange(c)[None, :]
    lower = (j <= t).astype(np.float32)
    blocks = [lower]
    for m in HGRN_SMALL:
        ref = (t // (2 * m)) * (2 * m) + m - 1
        blocks.append(lower - (j <= ref).astype(np.float32))
    cmat = np.concatenate(blocks, axis=0)
    cmat = np.concatenate([cmat, cmat], axis=1)
    masks = [np.eye(c, dtype=np.float32)]
    for m in HGRN_LEVELS:
        same_parent = (t // (2 * m)) == (j // (2 * m))
        masks.append((same_parent & (t % (2 * m) >= m) & (j % (2 * m) < m)).astype(np.float32))
    return cmat, np.stack(masks)


def _hgrn_kernel(q_ref, f_ref, v_ref, g_ref, lbl_ref, gn_ref, cmat_ref, mask_ref, o_ref,
                 st_ref, k_ref, gl_ref, oacc_ref, *, layer_e):
    t_rows = q_ref.shape[0]
    c = HGRN_CHUNK
    contract_lanes = (((1,), (1,)), ((), ()))

    @pl.when(pl.program_id(1) == 0)
    def _():
        st_ref[...] = jnp.zeros_like(st_ref)

    logits = lbl_ref[...]
    ex = jnp.exp(logits - jnp.max(logits, axis=0, keepdims=True))
    sm = ex / jnp.sum(ex, axis=0, keepdims=True)
    lb = jnp.sum(sm[:layer_e + 1], axis=0, keepdims=True) - sm[0:1]

    fx = f_ref[...]
    sig = 1.0 / (1.0 + jnp.exp(-fx))
    gl_ref[...] = jnp.log2(jnp.maximum(lb + (1.0 - lb) * sig, F_MIN))
    k_ref[...] = (1.0 - lb) * (1.0 - sig)

    def exp2_neg_abs(d):
        return jnp.exp2(-jnp.abs(d))

    row8 = lax.broadcasted_iota(jnp.int32, (c, A_HEAD_DIM), 0) % SUBLANES

    states = [st_ref[h] for h in range(A_HEADS)]

    def chunk(ci):
        rows = slice(ci * c, (ci + 1) * c)
        g = gl_ref[rows, :]
        g_hi = g.astype(BF16)
        g_lo = (g - g_hi.astype(F32)).astype(BF16)
        dall = jnp.dot(cmat_ref[...], jnp.concatenate([g_hi, g_lo], axis=0),
                       preferred_element_type=F32)
        for h in range(A_HEADS):
            cols = slice(h * A_HEAD_DIM, (h + 1) * A_HEAD_DIM)
            q = q_ref[rows, cols]
            k = k_ref[rows, cols]
            v = v_ref[rows, cols]
            b = dall[0:c, cols]
            level_x = []
            for li, m in enumerate(HGRN_LEVELS):
                if m < SUBLANES:
                    si = HGRN_SMALL.index(m) + 1
                    d = dall[si * c:(si + 1) * c, cols]
                    qk = jnp.where(row8 % (2 * m) >= m, q, k)
                else:
                    d_parts, qk_parts = [], []
                    for lo in range(0, c, 2 * m):
                        ref = lo + m - 1
                        d_parts.append(b[lo:lo + 2 * m] - b[ref:ref + 1])
                        qk_parts += [k[lo:lo + m], q[lo + m:lo + 2 * m]]
                    d = jnp.concatenate(d_parts, axis=0)
                    qk = jnp.concatenate(qk_parts, axis=0)
                level_x.append(qk * exp2_neg_abs(d))
            diag = jnp.sum(q * k, axis=-1, keepdims=True)
            a_rows = [diag[r0:r0 + SUBLANES] * mask_ref[0, r0:r0 + SUBLANES, :]
                      for r0 in range(0, c, SUBLANES)]
            for li, (m, x) in enumerate(zip(HGRN_LEVELS, level_x)):
                second = [r0 for r0 in range(0, c, SUBLANES) if m < SUBLANES or r0 % (2 * m) >= m]
                x_rows = x if len(second) == len(a_rows) else jnp.concatenate(
                    [x[r0:r0 + SUBLANES] for r0 in second], axis=0)
                gram = jnp.dot(x_rows.astype(BF16), x.T.astype(BF16), preferred_element_type=F32)
                for i, r0 in enumerate(second):
                    a_rows[r0 // SUBLANES] += (gram[i * SUBLANES:(i + 1) * SUBLANES]
                                               * mask_ref[li + 1, r0:r0 + SUBLANES, :])
            a = jnp.concatenate(a_rows, axis=0)
            b_end = b[c - 1:c, :]
            q_dec = (q * jnp.exp2(b)).astype(BF16)
            k_dec = (k * jnp.exp2(b_end - b)).astype(BF16)
            v_t = v.T.astype(BF16)
            st = states[h]
            lhs = jnp.concatenate([a.astype(BF16), q_dec], axis=1)
            rhs_t = jnp.concatenate([v_t, st.astype(BF16)], axis=1)
            oacc_ref[rows, cols] = lax.dot_general(lhs, rhs_t, contract_lanes,
                                                   preferred_element_type=F32)
            states[h] = st * jnp.exp2(b_end) + jnp.dot(v_t, k_dec, preferred_element_type=F32)

    for ci in range(t_rows // c):
        chunk(ci)
    for h in range(A_HEADS):
        st_ref[h] = states[h]

    gn = gn_ref[...]
    for h in range(A_HEADS):
        cols = slice(h * A_HEAD_DIM, (h + 1) * A_HEAD_DIM)
        o = oacc_ref[:, cols]
        on = o * lax.rsqrt(jnp.mean(o * o, axis=-1, keepdims=True) + RMS_EPS) * gn[:, cols]
        g = g_ref[:, cols]
        o_ref[:, cols] = (on * (g * _sigmoid_tanh(g))).astype(o_ref.dtype)


def _hgrn(z, lb_logits, out_norm, layer_e, bsz, seq):
    n = z.shape[0]
    tpb = seq // T_HGRN
    spec = lambda c: pl.BlockSpec((T_HGRN, A_WIDTH), lambda b, t: (b * tpb + t, c))
    const = lambda b, t: (0, 0)
    cmat, masks = _hgrn_constants()
    return pl.pallas_call(
        functools.partial(_hgrn_kernel, layer_e=layer_e),
        out_shape=jax.ShapeDtypeStruct((n, A_WIDTH), BF16),
        grid=(bsz, tpb),
        in_specs=[spec(0), spec(1), spec(2), spec(3),
                  pl.BlockSpec(lb_logits.shape, const),
                  pl.BlockSpec((1, A_WIDTH), const),
                  pl.BlockSpec(cmat.shape, const),
                  pl.BlockSpec(masks.shape, lambda b, t: (0, 0, 0))],
        out_specs=pl.BlockSpec((T_HGRN, A_WIDTH), lambda b, t: (b * tpb + t, 0)),
        scratch_shapes=[pltpu.VMEM((A_HEADS, A_HEAD_DIM, A_HEAD_DIM), F32),
                        pltpu.VMEM((T_HGRN, A_WIDTH), F32),
                        pltpu.VMEM((T_HGRN, A_WIDTH), F32),
                        pltpu.VMEM((T_HGRN, A_WIDTH), F32)],
        compiler_params=_params(("parallel", "arbitrary")),
        name="hgrn2",
    )(z, z, z, z, lb_logits, out_norm, jnp.asarray(cmat, BF16), jnp.asarray(masks, F32))


def _gelu_tanh(x):
    c = 0.7978845608028654
    return x * (0.5 + 0.5 * jnp.tanh(x * (c + (c * 0.044715) * (x * x))))


def _rglru_steps(gate_ref, xb_ref, cw_ref, cb_ref, wg_ref, bg_ref, lam_ref, o_ref,
                 xext_ref, a_ref, u_ref, h_ref, *, first_tile):
    t_rows = xb_ref.shape[0]
    pad = SUBLANES
    rb = RGLRU_ROW_BLOCK
    held = {}

    @pl.when(first_tile)
    def _():
        xext_ref[0:pad, :] = jnp.zeros((pad, B_WIDTH), F32)
        h_ref[...] = jnp.zeros_like(h_ref)

    lam = lam_ref[...]
    softplus_neg_lam = jnp.maximum(-lam, 0.0) + jnp.log1p(jnp.exp(-jnp.abs(lam)))
    log_a_scale = -RG_C * softplus_neg_lam
    cw = cw_ref[...]

    def gates(r0):
        rows = slice(r0, r0 + rb)
        xb = xb_ref[rows, :]
        xext_ref[pad + r0:pad + r0 + rb, :] = xb
        xc = cb_ref[...] + cw[B_CONV - 1:B_CONV, :] * xb
        for j in range(B_CONV - 1):
            shift = B_CONV - 1 - j
            xc = xc + cw[j:j + 1, :] * xext_ref[pad + r0 - shift:pad + r0 - shift + rb, :]
        if r0 + rb == t_rows:
            xext_ref[0:pad, :] = xext_ref[t_rows:t_rows + pad, :]
        xcb = xc.astype(BF16)

        def gate(part):
            slabs = []
            for s0 in range(0, B_WIDTH, GATE_SLAB):
                w = wg_ref[s0:s0 + GATE_SLAB, part * B_WIDTH + s0:part * B_WIDTH + s0 + GATE_SLAB]
                slabs.append(jnp.dot(xcb[:, s0:s0 + GATE_SLAB], w, preferred_element_type=F32))
            pre = jnp.concatenate(slabs, axis=1) + bg_ref[:, part * B_WIDTH:(part + 1) * B_WIDTH]
            return _sigmoid_tanh(pre)

        r = gate(0)
        i = gate(1)
        a = jnp.exp(log_a_scale * r)
        a_ref[rows, :] = a
        y = jnp.maximum(1.0 - a * a, 0.0)
        u_ref[rows, :] = (y * lax.rsqrt(jnp.maximum(y, F_MIN))) * (i * xc)

    row = lax.broadcasted_iota(jnp.int32, (SUBLANES, B_WIDTH), 0)

    def scan(r0):
        h_prev = held.get("h")
        if h_prev is None:
            h_prev = h_ref[...]
        for g0 in range(r0, r0 + rb, SUBLANES):
            rows = slice(g0, g0 + SUBLANES)
            a_cum = a_ref[rows, :]
            h = u_ref[rows, :]
            for sh in (1, 2, 4):
                keep = row >= sh
                h = h + a_cum * jnp.where(keep, pltpu.roll(h, sh, axis=0), 0.0)
                a_cum = a_cum * jnp.where(keep, pltpu.roll(a_cum, sh, axis=0), 1.0)
            h = h + a_cum * h_prev
            u_ref[rows, :] = h
            h_prev = jnp.broadcast_to(h[SUBLANES - 1:SUBLANES, :], (SUBLANES, B_WIDTH))
        held["h"] = h_prev
        if r0 + rb == t_rows:
            h_ref[...] = h_prev

    def gate_out(r0):
        rows = slice(r0, r0 + rb)
        o_ref[rows, :] = (u_ref[rows, :] * _gelu_tanh(gate_ref[rows, :])).astype(o_ref.dtype)

    blocks = range(0, t_rows, rb)
    steps = [(3.0, functools.partial(gates, r0)) for r0 in blocks]
    steps += [(2.0, functools.partial(scan, r0)) for r0 in blocks]
    steps += [(1.0, functools.partial(gate_out, r0)) for r0 in blocks]
    return steps


def _rglru_mlp_kernel(gate_ref, xb_ref, cw_ref, cb_ref, wg_ref, bg_ref, lam_ref,
                      x_ref, ya_ref, woa_ref, wob_ref, g_ref, w1_ref, w2_ref, o_ref,
                      xext_ref, a_ref, u_ref, h_ref, yb_ref, act_ref, *, tiles_per_seq):
    i = pl.program_id(0)

    @pl.when(i == 0)
    def _():
        yb_ref[...] = jnp.zeros_like(yb_ref)

    _residual_outproj(x_ref, [ya_ref, yb_ref], [woa_ref, wob_ref], o_ref)
    rglru_steps = _rglru_steps(gate_ref, xb_ref, cw_ref, cb_ref, wg_ref, bg_ref, lam_ref, yb_ref,
                               xext_ref, a_ref, u_ref, h_ref, first_tile=i % tiles_per_seq == 0)
    _run_interleaved(rglru_steps, _mlp_steps(g_ref, w1_ref, w2_ref, o_ref, act_ref))


def _rglru_mlp(x2, z, ya, conv_w, conv_b, w_gates_bf16, b_gates, lam, w_out_stack, mix_layer,
               gain, w1_stack, w2_stack, layer, seq):
    n, d = x2.shape
    t = T_TAIL
    n_tiles = n // t
    cur = lambda c: (lambda i: (jnp.minimum(i, n_tiles - 1), c))
    prev = lambda i: (jnp.maximum(i - 1, 0), 0)
    const = lambda i: (0, 0)
    once = pl.Buffered(1)
    return pl.pallas_call(
        functools.partial(_rglru_mlp_kernel, tiles_per_seq=seq // t),
        out_shape=jax.ShapeDtypeStruct((n, d), F32),
        grid=(n_tiles + 1,),
        in_specs=[pl.BlockSpec((t, B_WIDTH), cur(4)),
                  pl.BlockSpec((t, B_WIDTH), cur(5)),
                  pl.BlockSpec((B_CONV, B_WIDTH), const),
                  pl.BlockSpec((1, B_WIDTH), const),
                  pl.BlockSpec((B_WIDTH, 2 * B_WIDTH), const),
                  pl.BlockSpec((1, 2 * B_WIDTH), const),
                  pl.BlockSpec((1, B_WIDTH), const),
                  pl.BlockSpec((t, d), prev),
                  pl.BlockSpec((t, A_WIDTH), prev),
                  pl.BlockSpec((None, A_WIDTH, d), lambda i: (mix_layer, 0, 0), pipeline_mode=once),
                  pl.BlockSpec((None, B_WIDTH, d), lambda i: (mix_layer, 1, 0), pipeline_mode=once),
                  pl.BlockSpec((1, d), const),
                  pl.BlockSpec((None, d, D_FF), lambda i: (layer, 0, 0), pipeline_mode=once),
                  pl.BlockSpec((None, D_FF, d), lambda i: (layer, 0, 0), pipeline_mode=once)],
        out_specs=pl.BlockSpec((t, d), prev),
        scratch_shapes=[pltpu.VMEM((t + 2 * SUBLANES, B_WIDTH), F32),
                        pltpu.VMEM((t, B_WIDTH), F32),
                        pltpu.VMEM((t, B_WIDTH), F32),
                        pltpu.VMEM((SUBLANES, B_WIDTH), F32),
                        pltpu.VMEM((t, B_WIDTH), BF16),
                        pltpu.VMEM((t, D_FF), BF16)],
        compiler_params=_params(("arbitrary",)),
        name="rglru_mlp",
    )(z, z, conv_w, conv_b, w_gates_bf16, b_gates, lam, x2, ya, w_out_stack, w_out_stack,
      gain, w1_stack, w2_stack)


def _rope_kernel(pos_ref, invf_ref, sign_ref, cos_ref, sin_ref):
    ang = pos_ref[...].astype(F32) * invf_ref[...]
    cos_ref[...] = jnp.cos(ang)
    sin_ref[...] = jnp.sin(ang) * sign_ref[...]


def _rope_tables(pos_col, invf_tile, sign_tile):
    n = pos_col.shape[0]
    const = lambda i: (0, 0)
    return pl.pallas_call(
        _rope_kernel,
        out_shape=(jax.ShapeDtypeStruct((n, LANES), F32), jax.ShapeDtypeStruct((n, LANES), F32)),
        grid=(n // T_ROPE,),
        in_specs=[pl.BlockSpec((T_ROPE, 1), lambda i: (i, 0)),
                  pl.BlockSpec((1, LANES), const),
                  pl.BlockSpec((1, LANES), const)],
        out_specs=(pl.BlockSpec((T_ROPE, LANES), lambda i: (i, 0)),
                   pl.BlockSpec((T_ROPE, LANES), lambda i: (i, 0))),
        compiler_params=_params(("parallel",)),
        name="rope_tables",
    )(pos_col, invf_tile, sign_tile)


def _attn_steps(sinks_ref, q_ref, k_ref, v_ref, cos_ref, sin_ref, qg_ref, kg_ref, seg_ref, o_ref,
                qlo_ref, qhi_ref, kd_ref, vd_ref, *, first_tile):
    t_rows = q_ref.shape[0]
    n_blocks = t_rows // WINDOW

    half = C_HEAD_DIM // 2
    lane = lax.broadcasted_iota(jnp.int32, (t_rows, LANES), 1)
    low_half = lane < C_HEAD_DIM
    head_a = (lane // half) % 2 == 0
    cos = cos_ref[...]
    sin = sin_ref[...]
    seg = seg_ref[...]

    def norm_rope(xs, gain):
        sq = jnp.concatenate([x * x for x in xs], axis=1).astype(BF16)
        ss = jnp.dot(sq, seg, preferred_element_type=F32)
        out = []
        for g, x in enumerate(xs):
            xn = x * lax.rsqrt(ss[:, g * LANES:(g + 1) * LANES] * (1.0 / C_HEAD_DIM) + RMS_EPS) * gain
            out.append(xn * cos + pltpu.roll(xn, LANES // 2, axis=1) * sin)
        return out

    @pl.when(first_tile)
    def _():
        kd_ref[:, 0:WINDOW, :] = jnp.zeros((C_KV_HEADS, WINDOW, LANES), BF16)
        vd_ref[:, 0:WINDOW, 0:LANES] = jnp.zeros((C_KV_HEADS, WINDOW, LANES), BF16)
        vd_ref[:, :, LANES:] = jnp.ones((C_KV_HEADS, t_rows + WINDOW, LANES), BF16)

    @pl.when(jnp.logical_not(first_tile))
    def _():
        kd_ref[:, 0:WINDOW, :] = kd_ref[:, t_rows:t_rows + WINDOW, :]
        vd_ref[:, 0:WINDOW, :] = vd_ref[:, t_rows:t_rows + WINDOW, :]

    def kv_prep():
        k_groups = norm_rope([k_ref[:, 0:LANES].astype(F32), k_ref[:, LANES:2 * LANES].astype(F32)],
                             kg_ref[...])
        for pair, kr in enumerate(k_groups):
            vr = v_ref[:, pair * LANES:(pair + 1) * LANES].astype(F32)
            v_sw = pltpu.roll(vr, C_HEAD_DIM, axis=1)
            kd_ref[2 * pair, WINDOW:, :] = jnp.where(
                head_a, kr, pltpu.roll(kr, half, axis=1)).astype(BF16)
            kd_ref[2 * pair + 1, WINDOW:, :] = jnp.where(
                head_a, pltpu.roll(kr, LANES - half, axis=1), kr).astype(BF16)
            vd_ref[2 * pair, WINDOW:, 0:LANES] = jnp.where(low_half, vr, v_sw).astype(BF16)
            vd_ref[2 * pair + 1, WINDOW:, 0:LANES] = jnp.where(low_half, v_sw, vr).astype(BF16)

    scale = (C_HEAD_DIM ** -0.5) * LOG2_E

    def q_prep(hp0):
        groups = [q_ref[:, (hp0 + g) * LANES:(hp0 + g + 1) * LANES].astype(F32) for g in range(2)]
        for g, qr in enumerate(norm_rope(groups, qg_ref[...])):
            qr = qr * scale
            qlo_ref[hp0 + g] = jnp.where(head_a, qr, 0.0).astype(BF16)
            qhi_ref[hp0 + g] = jnp.where(head_a, 0.0, qr).astype(BF16)

    qi = lax.broadcasted_iota(jnp.int32, (2 * WINDOW, WINDOW), 0) % WINDOW
    own_key = lax.broadcasted_iota(jnp.int32, (2 * WINDOW, WINDOW), 1) <= qi
    upper_rows = lax.broadcasted_iota(jnp.int32, (2 * WINDOW, 1), 0) < WINDOW
    out_low = lax.broadcasted_iota(jnp.int32, (WINDOW, LANES), 1) < C_HEAD_DIM

    def head_pair(r0, hp, prev_is_padding):
        kvh = (2 * hp) // C_GROUP
        lhs = jnp.concatenate([qlo_ref[hp, pl.ds(r0, WINDOW), :],
                               qhi_ref[hp, pl.ds(r0, WINDOW), :]], axis=0)
        keys = kd_ref[kvh, pl.ds(r0, 2 * WINDOW), :]
        vals = vd_ref[kvh, pl.ds(r0, 2 * WINDOW), :]
        s = lax.dot_general(lhs, keys, (((1,), (1,)), ((), ())), preferred_element_type=F32)
        s_prev = s[:, :WINDOW]
        if prev_is_padding is not None:
            s_prev = jnp.where(prev_is_padding, NEG_BIG, s_prev)
        s = jnp.where(own_key, s[:, WINDOW:], s_prev)
        sink = jnp.where(upper_rows, sinks_ref[2 * hp] * LOG2_E, sinks_ref[2 * hp + 1] * LOG2_E)
        m = jnp.maximum(jnp.max(s, axis=-1, keepdims=True), sink)
        p = jnp.exp2(s - m)
        p = jnp.concatenate([jnp.where(own_key, 0.0, p), jnp.where(own_key, p, 0.0)], axis=1)
        pv_sum = jnp.dot(p.astype(BF16), vals, preferred_element_type=F32)
        pv = pv_sum[:, :LANES] / (pv_sum[:, LANES:] + jnp.exp2(sink - m))
        o_ref[pl.ds(r0, WINDOW), hp * LANES:(hp + 1) * LANES] = jnp.where(
            out_low, pv[:WINDOW], pv[WINDOW:]).astype(o_ref.dtype)

    steps = [(4.0, kv_prep)]
    steps += [(2.0, functools.partial(q_prep, hp0)) for hp0 in range(0, C_HEADS // 2, 2)]
    for j in range(n_blocks):
        prev_is_padding = first_tile if j == 0 else None
        steps += [(1.0, functools.partial(head_pair, j * WINDOW, hp, prev_is_padding))
                  for hp in range(C_HEADS // 2)]
    return steps


def _attn_mlp_kernel(sinks_ref, q_ref, k_ref, v_ref, cos_ref, sin_ref, qg_ref, kg_ref, seg_ref,
                     x_ref, wo_ref, g_ref, w1_ref, w2_ref, o_ref,
                     qlo_ref, qhi_ref, kd_ref, vd_ref, attn_ref, act_ref, *, tiles_per_seq):
    i = pl.program_id(0)

    @pl.when(i == 0)
    def _():
        attn_ref[...] = jnp.zeros_like(attn_ref)

    _residual_outproj(x_ref, [attn_ref], [wo_ref], o_ref)
    attn_steps = _attn_steps(sinks_ref, q_ref, k_ref, v_ref, cos_ref, sin_ref, qg_ref, kg_ref, seg_ref,
                             attn_ref, qlo_ref, qhi_ref, kd_ref, vd_ref,
                             first_tile=i % tiles_per_seq == 0)
    _run_interleaved(attn_steps, _mlp_steps(g_ref, w1_ref, w2_ref, o_ref, act_ref))


def _attn_mlp(x2, z, cos_t, sin_t, q_gain_tile, k_gain_tile, seg_ones, sinks, w_out_stack, mix_layer,
              gain, w1_stack, w2_stack, layer, seq):
    n, d = x2.shape
    n_tiles = n // T_TAIL
    kv_w = C_KV_HEADS * C_HEAD_DIM
    cur = lambda c: (lambda i: (jnp.minimum(i, n_tiles - 1), c))
    prev = lambda i: (jnp.maximum(i - 1, 0), 0)
    const = lambda i: (0, 0)
    once = pl.Buffered(1)
    return pl.pallas_call(
        functools.partial(_attn_mlp_kernel, tiles_per_seq=seq // T_TAIL),
        out_shape=jax.ShapeDtypeStruct((n, d), F32),
        grid=(n_tiles + 1,),
        in_specs=[pl.BlockSpec(memory_space=pltpu.SMEM),
                  pl.BlockSpec((T_TAIL, D_MODEL), cur(0)),
                  pl.BlockSpec((T_TAIL, kv_w), cur(D_MODEL // kv_w)),
                  pl.BlockSpec((T_TAIL, kv_w), cur(D_MODEL // kv_w + 1)),
                  pl.BlockSpec((T_TAIL, LANES), cur(0)),
                  pl.BlockSpec((T_TAIL, LANES), cur(0)),
                  pl.BlockSpec((1, LANES), const),
                  pl.BlockSpec((1, LANES), const),
                  pl.BlockSpec((2 * LANES, 2 * LANES), const),
                  pl.BlockSpec((T_TAIL, d), prev),
                  pl.BlockSpec((None, d, d), lambda i: (mix_layer, 0, 0), pipeline_mode=once),
                  pl.BlockSpec((1, d), const),
                  pl.BlockSpec((None, d, D_FF), lambda i: (layer, 0, 0), pipeline_mode=once),
                  pl.BlockSpec((None, D_FF, d), lambda i: (layer, 0, 0), pipeline_mode=once)],
        out_specs=pl.BlockSpec((T_TAIL, d), prev),
        scratch_shapes=[pltpu.VMEM((C_HEADS // 2, T_TAIL, LANES), BF16),
                        pltpu.VMEM((C_HEADS // 2, T_TAIL, LANES), BF16),
                        pltpu.VMEM((C_KV_HEADS, T_TAIL + WINDOW, LANES), BF16),
                        pltpu.VMEM((C_KV_HEADS, T_TAIL + WINDOW, 2 * LANES), BF16),
                        pltpu.VMEM((T_TAIL, d), BF16),
                        pltpu.VMEM((T_TAIL, D_FF), BF16)],
        compiler_params=_params(("arbitrary",)),
        name="swa_mlp",
    )(sinks, z, z, z, cos_t, sin_t, q_gain_tile, k_gain_tile, seg_ones,
      x2, w_out_stack, gain, w1_stack, w2_stack)


def _block_diag(w):
    nb, bi, bj = w.shape
    eye = jnp.eye(nb, dtype=w.dtype)
    return (eye[:, None, :, None] * w[:, :, None, :]).reshape(nb * bi, nb * bj)


def _rotary_pair_layout(t):
    lead = t.shape[:-1]
    half = C_HEAD_DIM // 2
    t = t.reshape(*lead, -1, 2, 2, half)
    return jnp.swapaxes(t, -2, -3).reshape(*lead, -1)


def kernel(x, positions, norm_mix, norm_mlp, w_mlp_in, w_mlp_out, w_in_even, w_out_even,
           hgrn_lb_logits, hgrn_out_norm, conv_w, conv_b, rg_wa, rg_ba, rg_wx, rg_bx, rg_lambda,
           w_in_odd, w_out_odd, q_norm, k_norm, sinks):
    bsz, seq, d = x.shape
    n = bsz * seq
    depth = norm_mix.shape[0]
    x2 = x.reshape(n, d)

    half = C_HEAD_DIM // 2
    inv_freq = ROPE_THETA ** (-jnp.arange(0, C_HEAD_DIM, 2, dtype=F32) / C_HEAD_DIM)
    invf_tile = jnp.tile(inv_freq, LANES // half).reshape(1, LANES)
    sign_tile = jnp.concatenate([-jnp.ones((LANES // 2,), F32), jnp.ones((LANES // 2,), F32)]).reshape(1, LANES)
    cos_t, sin_t = _rope_tables(positions.reshape(n, 1), invf_tile, sign_tile)
    lane_head = 2 * (jnp.arange(2 * LANES) // LANES) + (jnp.arange(2 * LANES) // half) % 2
    seg_ones = (lane_head[:, None] == lane_head[None, :]).astype(BF16)

    w_mlp_in_b = w_mlp_in.astype(BF16)
    w_mlp_out_b = w_mlp_out.astype(BF16)
    w_in_even_b = w_in_even.astype(BF16)
    w_out_even_b = w_out_even.astype(BF16)
    q_cols = C_HEADS * C_HEAD_DIM
    qk_cols = q_cols + C_KV_HEADS * C_HEAD_DIM
    w_in_odd_b = jnp.concatenate([_rotary_pair_layout(w_in_odd[..., :q_cols]),
                                  _rotary_pair_layout(w_in_odd[..., q_cols:qk_cols]),
                                  w_in_odd[..., qk_cols:]], axis=-1).astype(BF16)
    w_out_odd_b = w_out_odd.astype(BF16)

    for layer in range(depth):
        gain_mix = norm_mix[layer].reshape(1, d)
        gain_mlp = norm_mlp[layer].reshape(1, d)
        if layer % 2 == 0:
            e = layer // 2
            z = _inproj(x2, gain_mix, w_in_even_b, e, F32)
            ya = _hgrn(z, hgrn_lb_logits, hgrn_out_norm[e].reshape(1, A_WIDTH), e, bsz, seq)
            w_gates = jnp.concatenate([_block_diag(rg_wa[e]), _block_diag(rg_wx[e])], axis=1).astype(BF16)
            b_gates = jnp.concatenate([rg_ba[e], rg_bx[e]]).reshape(1, 2 * B_WIDTH)
            x2 = _rglru_mlp(x2, z, ya, conv_w[e], conv_b[e].reshape(1, B_WIDTH), w_gates, b_gates,
                            rg_lambda[e].reshape(1, B_WIDTH), w_out_even_b, e,
                            gain_mlp, w_mlp_in_b, w_mlp_out_b, layer, seq)
        else:
            o = layer // 2
            z = _inproj(x2, gain_mix, w_in_odd_b, o, BF16)
            q_gain = _rotary_pair_layout(jnp.tile(q_norm[o], LANES // C_HEAD_DIM)).reshape(1, LANES)
            k_gain = _rotary_pair_layout(jnp.tile(k_norm[o], LANES // C_HEAD_DIM)).reshape(1, LANES)
            x2 = _attn_mlp(x2, z, cos_t, sin_t, q_gain, k_gain, seg_ones, sinks[o], w_out_odd_b, o,
                           gain_mlp, w_mlp_in_b, w_mlp_out_b, layer, seq)
    return x2.reshape(bsz, seq, d)
```

```python
import functools

import numpy as np
import jax
import jax.numpy as jnp
from jax import lax
from jax.experimental import pallas as pl
from jax.experimental.pallas import tpu as pltpu

F32 = jnp.float32
BF16 = jnp.bfloat16

D_MODEL = 1024
D_FF = 4 * D_MODEL
RMS_EPS = 1e-6
NEG_BIG = -1e30
F_MIN = 1e-30
A_WIDTH = D_MODEL // 2
A_HEAD_DIM = 128
A_HEADS = A_WIDTH // A_HEAD_DIM
B_WIDTH = D_MODEL // 2
B_BLOCKS = 8
B_BLOCK_DIM = B_WIDTH // B_BLOCKS
B_CONV = 4
RG_C = 8.0
C_HEAD_DIM = 64
C_HEADS = D_MODEL // C_HEAD_DIM
C_KV_HEADS = 4
C_GROUP = C_HEADS // C_KV_HEADS
WINDOW = 128
ROPE_THETA = 10000.0
LOG2_E = 1.4426950408889634
EVEN_IN = 4 * A_WIDTH + 2 * B_WIDTH
ODD_IN = (C_HEADS + 2 * C_KV_HEADS) * C_HEAD_DIM

LANES = 128
SUBLANES = 8
VMEM_LIMIT_BYTES = 56 * 1024 * 1024

TM_PROJ = 1024
T_HGRN = 512
T_TAIL = 512
TAIL_ROW_BLOCK = 256
T_ROPE = 1024
RGLRU_ROW_BLOCK = 128
GATE_SLAB = 256


def _params(semantics):
    return pltpu.CompilerParams(dimension_semantics=semantics, vmem_limit_bytes=VMEM_LIMIT_BYTES)


def _rmsnorm(xf, gain):
    ms = jnp.mean(xf * xf, axis=-1, keepdims=True)
    return xf * lax.rsqrt(ms + RMS_EPS) * gain


def _sigmoid_tanh(x):
    return 0.5 * jnp.tanh(0.5 * x) + 0.5


def _inproj_kernel(x_ref, g_ref, w_ref, o_ref, *, chunk):
    h = _rmsnorm(x_ref[...], g_ref[...]).astype(BF16)
    n_out = o_ref.shape[1]
    for c in range(n_out // chunk):
        cols = slice(c * chunk, (c + 1) * chunk)
        o_ref[:, cols] = jnp.dot(h, w_ref[:, cols], preferred_element_type=F32).astype(o_ref.dtype)


def _inproj(x2, gain, w_stack_bf16, layer, out_dtype):
    n, d = x2.shape
    n_out = w_stack_bf16.shape[2]
    chunk = 512
    return pl.pallas_call(
        functools.partial(_inproj_kernel, chunk=chunk),
        out_shape=jax.ShapeDtypeStruct((n, n_out), out_dtype),
        grid=(n // TM_PROJ,),
        in_specs=[
            pl.BlockSpec((TM_PROJ, d), lambda i: (i, 0)),
            pl.BlockSpec((1, d), lambda i: (0, 0)),
            pl.BlockSpec((None, d, n_out), lambda i: (layer, 0, 0), pipeline_mode=pl.Buffered(1)),
        ],
        out_specs=pl.BlockSpec((TM_PROJ, n_out), lambda i: (i, 0)),
        compiler_params=_params(("parallel",)),
        name="inproj",
    )(x2, gain, w_stack_bf16)


def _residual_outproj(x_ref, mix_refs, wo_refs, o_ref):
    for r0 in range(0, o_ref.shape[0], TAIL_ROW_BLOCK):
        rows = slice(r0, r0 + TAIL_ROW_BLOCK)
        x1 = x_ref[rows, :]
        for m_ref, wo_ref in zip(mix_refs, wo_refs):
            x1 = x1 + jnp.dot(m_ref[rows, :], wo_ref[...], preferred_element_type=F32)
        o_ref[rows, :] = x1


def _mlp_steps(g_ref, w1_ref, w2_ref, o_ref, act_ref):
    held = {}
    up_cols, down_cols = 512, 256
    row_blocks = range(0, o_ref.shape[0], TAIL_ROW_BLOCK)

    def norm(r0):
        rows = slice(r0, r0 + TAIL_ROW_BLOCK)
        held[r0] = _rmsnorm(o_ref[rows, :], g_ref[...]).astype(BF16)

    def up(c, r0):
        rows = slice(r0, r0 + TAIL_ROW_BLOCK)
        cols = slice(c * up_cols, (c + 1) * up_cols)
        u = jnp.maximum(jnp.dot(held[r0], w1_ref[:, cols], preferred_element_type=F32), 0.0)
        act_ref[rows, cols] = (u * u).astype(BF16)

    def down(c, r0):
        rows = slice(r0, r0 + TAIL_ROW_BLOCK)
        cols = slice(c * down_cols, (c + 1) * down_cols)
        o_ref[rows, cols] += jnp.dot(act_ref[rows, :], w2_ref[:, cols], preferred_element_type=F32)

    steps = [(0.1, functools.partial(norm, r0)) for r0 in row_blocks]
    steps += [(0.5, functools.partial(up, c, r0)) for c in range(D_FF // up_cols) for r0 in row_blocks]
    steps += [(0.5, functools.partial(down, c, r0))
              for c in range(D_MODEL // down_cols) for r0 in row_blocks]
    return steps


def _run_interleaved(a_steps, b_steps):
    a_total = sum(c for c, _ in a_steps)
    b_total = sum(c for c, _ in b_steps)
    ia = ib = 0
    a_done = b_done = 0.0
    while ia < len(a_steps) or ib < len(b_steps):
        take_a = ib >= len(b_steps) or (ia < len(a_steps) and a_done / a_total <= b_done / b_total)
        if take_a:
            a_done += a_steps[ia][0]
            a_steps[ia][1]()
            ia += 1
        else:
            b_done += b_steps[ib][0]
            b_steps[ib][1]()
            ib += 1


HGRN_CHUNK = 128
HGRN_LEVELS = (1, 2, 4, 8, 16, 32, 64)
HGRN_SMALL = tuple(m for m in HGRN_LEVELS if m < SUBLANES)


def _hgrn_constants():
    c = HGRN_CHUNK
    t = np.arange(c)[:, None]
    j = np.arange(c)[None, :]
    lower = (j <= t).astype(np.float32)
    blocks = [lower]
    for m in HGRN_SMALL:
        ref = (t // (2 * m)) * (2 * m) + m - 1
        blocks.append(lower - (j <= ref).astype(np.float32))
    cmat = np.concatenate(blocks, axis=0)
    cmat = np.concatenate([cmat, cmat], axis=1)
    masks = [np.eye(c, dtype=np.float32)]
    for m in HGRN_LEVELS:
        same_parent = (t // (2 * m)) == (j // (2 * m))
        masks.append((same_parent & (t % (2 * m) >= m) & (j % (2 * m) < m)).astype(np.float32))
    return cmat, np.stack(masks)


def _hgrn_kernel(q_ref, f_ref, v_ref, g_ref, lbl_ref, gn_ref, cmat_ref, mask_ref, o_ref,
                 st_ref, k_ref, gl_ref, oacc_ref, *, layer_e):
    t_rows = q_ref.shape[0]
    c = HGRN_CHUNK
    contract_lanes = (((1,), (1,)), ((), ()))

    @pl.when(pl.program_id(1) == 0)
    def _():
        st_ref[...] = jnp.zeros_like(st_ref)

    logits = lbl_ref[...]
    ex = jnp.exp(logits - jnp.max(logits, axis=0, keepdims=True))
    sm = ex / jnp.sum(ex, axis=0, keepdims=True)
    lb = jnp.sum(sm[:layer_e + 1], axis=0, keepdims=True) - sm[0:1]

    fx = f_ref[...]
    sig = 1.0 / (1.0 + jnp.exp(-fx))
    gl_ref[...] = jnp.log2(jnp.maximum(lb + (1.0 - lb) * sig, F_MIN))
    k_ref[...] = (1.0 - lb) * (1.0 - sig)

    def exp2_neg_abs(d):
        return jnp.exp2(-jnp.abs(d))

    row8 = lax.broadcasted_iota(jnp.int32, (c, A_HEAD_DIM), 0) % SUBLANES

    states = [st_ref[h] for h in range(A_HEADS)]

    def chunk(ci):
        rows = slice(ci * c, (ci + 1) * c)
        g = gl_ref[rows, :]
        g_hi = g.astype(BF16)
        g_lo = (g - g_hi.astype(F32)).astype(BF16)
        dall = jnp.dot(cmat_ref[...], jnp.concatenate([g_hi, g_lo], axis=0),
                       preferred_element_type=F32)
        for h in range(A_HEADS):
            cols = slice(h * A_HEAD_DIM, (h + 1) * A_HEAD_DIM)
            q = q_ref[rows, cols]
            k = k_ref[rows, cols]
            v = v_ref[rows, cols]
            b = dall[0:c, cols]
            level_x = []
            for li, m in enumerate(HGRN_LEVELS):
                if m < SUBLANES:
                    si = HGRN_SMALL.index(m) + 1
                    d = dall[si * c:(si + 1) * c, cols]
                    qk = jnp.where(row8 % (2 * m) >= m, q, k)
                else:
                    d_parts, qk_parts = [], []
                    for lo in range(0, c, 2 * m):
                        ref = lo + m - 1
                        d_parts.append(b[lo:lo + 2 * m] - b[ref:ref + 1])
                        qk_parts += [k[lo:lo + m], q[lo + m:lo + 2 * m]]
                    d = jnp.concatenate(d_parts, axis=0)
                    qk = jnp.concatenate(qk_parts, axis=0)
                level_x.append(qk * exp2_neg_abs(d))
            diag = jnp.sum(q * k, axis=-1, keepdims=True)
            a_rows = [diag[r0:r0 + SUBLANES] * mask_ref[0, r0:r0 + SUBLANES, :]
                      for r0 in range(0, c, SUBLANES)]
            for li, (m, x) in enumerate(zip(HGRN_LEVELS, level_x)):
                second = [r0 for r0 in range(0, c, SUBLANES) if m < SUBLANES or r0 % (2 * m) >= m]
                x_rows = x if len(second) == len(a_rows) else jnp.concatenate(
                    [x[r0:r0 + SUBLANES] for r0 in second], axis=0)
                gram = jnp.dot(x_rows.astype(BF16), x.T.astype(BF16), preferred_element_type=F32)
                for i, r0 in enumerate(second):
                    a_rows[r0 // SUBLANES] += (gram[i * SUBLANES:(i + 1) * SUBLANES]
                                               * mask_ref[li + 1, r0:r0 + SUBLANES, :])
            a = jnp.concatenate(a_rows, axis=0)
            b_end = b[c - 1:c, :]
            q_dec = (q * jnp.exp2(b)).astype(BF16)
            k_dec = (k * jnp.exp2(b_end - b)).astype(BF16)
            v_t = v.T.astype(BF16)
            st = states[h]
            lhs = jnp.concatenate([a.astype(BF16), q_dec], axis=1)
            rhs_t = jnp.concatenate([v_t, st.astype(BF16)], axis=1)
            oacc_ref[rows, cols] = lax.dot_general(lhs, rhs_t, contract_lanes,
                                                   preferred_element_type=F32)
            states[h] = st * jnp.exp2(b_end) + jnp.dot(v_t, k_dec, preferred_element_type=F32)

    for ci in range(t_rows // c):
        chunk(ci)
    for h in range(A_HEADS):
        st_ref[h] = states[h]

    gn = gn_ref[...]
    for h in range(A_HEADS):
        cols = slice(h * A_HEAD_DIM, (h + 1) * A_HEAD_DIM)
        o = oacc_ref[:, cols]
        on = o * lax.rsqrt(jnp.mean(o * o, axis=-1, keepdims=True) + RMS_EPS) * gn[:, cols]
        g = g_ref[:, cols]
        o_ref[:, cols] = (on * (g * _sigmoid_tanh(g))).astype(o_ref.dtype)


def _hgrn(z, lb_logits, out_norm, layer_e, bsz, seq):
    n = z.shape[0]
    tpb = seq // T_HGRN
    spec = lambda c: pl.BlockSpec((T_HGRN, A_WIDTH), lambda b, t: (b * tpb + t, c))
    const = lambda b, t: (0, 0)
    cmat, masks = _hgrn_constants()
    return pl.pallas_call(
        functools.partial(_hgrn_kernel, layer_e=layer_e),
        out_shape=jax.ShapeDtypeStruct((n, A_WIDTH), BF16),
        grid=(bsz, tpb),
        in_specs=[spec(0), spec(1), spec(2), spec(3),
                  pl.BlockSpec(lb_logits.shape, const),
                  pl.BlockSpec((1, A_WIDTH), const),
                  pl.BlockSpec(cmat.shape, const),
                  pl.BlockSpec(masks.shape, lambda b, t: (0, 0, 0))],
        out_specs=pl.BlockSpec((T_HGRN, A_WIDTH), lambda b, t: (b * tpb + t, 0)),
        scratch_shapes=[pltpu.VMEM((A_HEADS, A_HEAD_DIM, A_HEAD_DIM), F32),
                        pltpu.VMEM((T_HGRN, A_WIDTH), F32),
                        pltpu.VMEM((T_HGRN, A_WIDTH), F32),
                        pltpu.VMEM((T_HGRN, A_WIDTH), F32)],
        compiler_params=_params(("parallel", "arbitrary")),
        name="hgrn2",
    )(z, z, z, z, lb_logits, out_norm, jnp.asarray(cmat, BF16), jnp.asarray(masks, F32))


def _gelu_tanh(x):
    c = 0.7978845608028654
    return x * (0.5 + 0.5 * jnp.tanh(x * (c + (c * 0.044715) * (x * x))))


def _rglru_steps(gate_ref, xb_ref, cw_ref, cb_ref, wg_ref, bg_ref, lam_ref, o_ref,
                 xext_ref, a_ref, u_ref, h_ref, *, first_tile):
    t_rows = xb_ref.shape[0]
    pad = SUBLANES
    rb = RGLRU_ROW_BLOCK
    held = {}

    @pl.when(first_tile)
    def _():
        xext_ref[0:pad, :] = jnp.zeros((pad, B_WIDTH), F32)
        h_ref[...] = jnp.zeros_like(h_ref)

    lam = lam_ref[...]
    softplus_neg_lam = jnp.maximum(-lam, 0.0) + jnp.log1p(jnp.exp(-jnp.abs(lam)))
    log_a_scale = -RG_C * softplus_neg_lam
    cw = cw_ref[...]

    def gates(r0):
        rows = slice(r0, r0 + rb)
        xb = xb_ref[rows, :]
        xext_ref[pad + r0:pad + r0 + rb, :] = xb
        xc = cb_ref[...] + cw[B_CONV - 1:B_CONV, :] * xb
        for j in range(B_CONV - 1):
            shift = B_CONV - 1 - j
            xc = xc + cw[j:j + 1, :] * xext_ref[pad + r0 - shift:pad + r0 - shift + rb, :]
        if r0 + rb == t_rows:
            xext_ref[0:pad, :] = xext_ref[t_rows:t_rows + pad, :]
        xcb = xc.astype(BF16)

        def gate(part):
            slabs = []
            for s0 in range(0, B_WIDTH, GATE_SLAB):
                w = wg_ref[s0:s0 + GATE_SLAB, part * B_WIDTH + s0:part * B_WIDTH + s0 + GATE_SLAB]
                slabs.append(jnp.dot(xcb[:, s0:s0 + GATE_SLAB], w, preferred_element_type=F32))
            pre = jnp.concatenate(slabs, axis=1) + bg_ref[:, part * B_WIDTH:(part + 1) * B_WIDTH]
            return _sigmoid_tanh(pre)

        r = gate(0)
        i = gate(1)
        a = jnp.exp(log_a_scale * r)
        a_ref[rows, :] = a
        y = jnp.maximum(1.0 - a * a, 0.0)
        u_ref[rows, :] = (y * lax.rsqrt(jnp.maximum(y, F_MIN))) * (i * xc)

    row = lax.broadcasted_iota(jnp.int32, (SUBLANES, B_WIDTH), 0)

    def scan(r0):
        h_prev = held.get("h")
        if h_prev is None:
            h_prev = h_ref[...]
        for g0 in range(r0, r0 + rb, SUBLANES):
            rows = slice(g0, g0 + SUBLANES)
            a_cum = a_ref[rows, :]
            h = u_ref[rows, :]
            for sh in (1, 2, 4):
                keep = row >= sh
                h = h + a_cum * jnp.where(keep, pltpu.roll(h, sh, axis=0), 0.0)
                a_cum = a_cum * jnp.where(keep, pltpu.roll(a_cum, sh, axis=0), 1.0)
            h = h + a_cum * h_prev
            u_ref[rows, :] = h
            h_prev = jnp.broadcast_to(h[SUBLANES - 1:SUBLANES, :], (SUBLANES, B_WIDTH))
        held["h"] = h_prev
        if r0 + rb == t_rows:
            h_ref[...] = h_prev

    def gate_out(r0):
        rows = slice(r0, r0 + rb)
        o_ref[rows, :] = (u_ref[rows, :] * _gelu_tanh(gate_ref[rows, :])).astype(o_ref.dtype)

    blocks = range(0, t_rows, rb)
    steps = [(3.0, functools.partial(gates, r0)) for r0 in blocks]
    steps += [(2.0, functools.partial(scan, r0)) for r0 in blocks]
    steps += [(1.0, functools.partial(gate_out, r0)) for r0 in blocks]
    return steps


def _rglru_mlp_kernel(gate_ref, xb_ref, cw_ref, cb_ref, wg_ref, bg_ref, lam_ref,
                      x_ref, ya_ref, woa_ref, wob_ref, g_ref, w1_ref, w2_ref, o_ref,
                      xext_ref, a_ref, u_ref, h_ref, yb_ref, act_ref, *, tiles_per_seq):
    i = pl.program_id(0)

    @pl.when(i == 0)
    def _():
        yb_ref[...] = jnp.zeros_like(yb_ref)

    _residual_outproj(x_ref, [ya_ref, yb_ref], [woa_ref, wob_ref], o_ref)
    rglru_steps = _rglru_steps(gate_ref, xb_ref, cw_ref, cb_ref, wg_ref, bg_ref, lam_ref, yb_ref,
                               xext_ref, a_ref, u_ref, h_ref, first_tile=i % tiles_per_seq == 0)
    _run_interleaved(rglru_steps, _mlp_steps(g_ref, w1_ref, w2_ref, o_ref, act_ref))


def _rglru_mlp(x2, z, ya, conv_w, conv_b, w_gates_bf16, b_gates, lam, w_out_stack, mix_layer,
               gain, w1_stack, w2_stack, layer, seq):
    n, d = x2.shape
    t = T_TAIL
    n_tiles = n // t
    cur = lambda c: (lambda i: (jnp.minimum(i, n_tiles - 1), c))
    prev = lambda i: (jnp.maximum(i - 1, 0), 0)
    const = lambda i: (0, 0)
    once = pl.Buffered(1)
    return pl.pallas_call(
        functools.partial(_rglru_mlp_kernel, tiles_per_seq=seq // t),
        out_shape=jax.ShapeDtypeStruct((n, d), F32),
        grid=(n_tiles + 1,),
        in_specs=[pl.BlockSpec((t, B_WIDTH), cur(4)),
                  pl.BlockSpec((t, B_WIDTH), cur(5)),
                  pl.BlockSpec((B_CONV, B_WIDTH), const),
                  pl.BlockSpec((1, B_WIDTH), const),
                  pl.BlockSpec((B_WIDTH, 2 * B_WIDTH), const),
                  pl.BlockSpec((1, 2 * B_WIDTH), const),
                  pl.BlockSpec((1, B_WIDTH), const),
                  pl.BlockSpec((t, d), prev),
                  pl.BlockSpec((t, A_WIDTH), prev),
                  pl.BlockSpec((None, A_WIDTH, d), lambda i: (mix_layer, 0, 0), pipeline_mode=once),
                  pl.BlockSpec((None, B_WIDTH, d), lambda i: (mix_layer, 1, 0), pipeline_mode=once),
                  pl.BlockSpec((1, d), const),
                  pl.BlockSpec((None, d, D_FF), lambda i: (layer, 0, 0), pipeline_mode=once),
                  pl.BlockSpec((None, D_FF, d), lambda i: (layer, 0, 0), pipeline_mode=once)],
        out_specs=pl.BlockSpec((t, d), prev),
        scratch_shapes=[pltpu.VMEM((t + 2 * SUBLANES, B_WIDTH), F32),
                        pltpu.VMEM((t, B_WIDTH), F32),
                        pltpu.VMEM((t, B_WIDTH), F32),
                        pltpu.VMEM((SUBLANES, B_WIDTH), F32),
                        pltpu.VMEM((t, B_WIDTH), BF16),
                        pltpu.VMEM((t, D_FF), BF16)],
        compiler_params=_params(("arbitrary",)),
        name="rglru_mlp",
    )(z, z, conv_w, conv_b, w_gates_bf16, b_gates, lam, x2, ya, w_out_stack, w_out_stack,
      gain, w1_stack, w2_stack)


def _rope_kernel(pos_ref, invf_ref, sign_ref, cos_ref, sin_ref):
    ang = pos_ref[...].astype(F32) * invf_ref[...]
    cos_ref[...] = jnp.cos(ang)
    sin_ref[...] = jnp.sin(ang) * sign_ref[...]


def _rope_tables(pos_col, invf_tile, sign_tile):
    n = pos_col.shape[0]
    const = lambda i: (0, 0)
    return pl.pallas_call(
        _rope_kernel,
        out_shape=(jax.ShapeDtypeStruct((n, LANES), F32), jax.ShapeDtypeStruct((n, LANES), F32)),
        grid=(n // T_ROPE,),
        in_specs=[pl.BlockSpec((T_ROPE, 1), lambda i: (i, 0)),
                  pl.BlockSpec((1, LANES), const),
                  pl.BlockSpec((1, LANES), const)],
        out_specs=(pl.BlockSpec((T_ROPE, LANES), lambda i: (i, 0)),
                   pl.BlockSpec((T_ROPE, LANES), lambda i: (i, 0))),
        compiler_params=_params(("parallel",)),
        name="rope_tables",
    )(pos_col, invf_tile, sign_tile)


def _attn_steps(sinks_ref, q_ref, k_ref, v_ref, cos_ref, sin_ref, qg_ref, kg_ref, seg_ref, o_ref,
                qlo_ref, qhi_ref, kd_ref, vd_ref, *, first_tile):
    t_rows = q_ref.shape[0]
    n_blocks = t_rows // WINDOW

    half = C_HEAD_DIM // 2
    lane = lax.broadcasted_iota(jnp.int32, (t_rows, LANES), 1)
    low_half = lane < C_HEAD_DIM
    head_a = (lane // half) % 2 == 0
    cos = cos_ref[...]
    sin = sin_ref[...]
    seg = seg_ref[...]

    def norm_rope(xs, gain):
        sq = jnp.concatenate([x * x for x in xs], axis=1).astype(BF16)
        ss = jnp.dot(sq, seg, preferred_element_type=F32)
        out = []
        for g, x in enumerate(xs):
            xn = x * lax.rsqrt(ss[:, g * LANES:(g + 1) * LANES] * (1.0 / C_HEAD_DIM) + RMS_EPS) * gain
            out.append(xn * cos + pltpu.roll(xn, LANES // 2, axis=1) * sin)
        return out

    @pl.when(first_tile)
    def _():
        kd_ref[:, 0:WINDOW, :] = jnp.zeros((C_KV_HEADS, WINDOW, LANES), BF16)
        vd_ref[:, 0:WINDOW, 0:LANES] = jnp.zeros((C_KV_HEADS, WINDOW, LANES), BF16)
        vd_ref[:, :, LANES:] = jnp.ones((C_KV_HEADS, t_rows + WINDOW, LANES), BF16)

    @pl.when(jnp.logical_not(first_tile))
    def _():
        kd_ref[:, 0:WINDOW, :] = kd_ref[:, t_rows:t_rows + WINDOW, :]
        vd_ref[:, 0:WINDOW, :] = vd_ref[:, t_rows:t_rows + WINDOW, :]

    def kv_prep():
        k_groups = norm_rope([k_ref[:, 0:LANES].astype(F32), k_ref[:, LANES:2 * LANES].astype(F32)],
                             kg_ref[...])
        for pair, kr in enumerate(k_groups):
            vr = v_ref[:, pair * LANES:(pair + 1) * LANES].astype(F32)
            v_sw = pltpu.roll(vr, C_HEAD_DIM, axis=1)
            kd_ref[2 * pair, WINDOW:, :] = jnp.where(
                head_a, kr, pltpu.roll(kr, half, axis=1)).astype(BF16)
            kd_ref[2 * pair + 1, WINDOW:, :] = jnp.where(
                head_a, pltpu.roll(kr, LANES - half, axis=1), kr).astype(BF16)
            vd_ref[2 * pair, WINDOW:, 0:LANES] = jnp.where(low_half, vr, v_sw).astype(BF16)
            vd_ref[2 * pair + 1, WINDOW:, 0:LANES] = jnp.where(low_half, v_sw, vr).astype(BF16)

    scale = (C_HEAD_DIM ** -0.5) * LOG2_E

    def q_prep(hp0):
        groups = [q_ref[:, (hp0 + g) * LANES:(hp0 + g + 1) * LANES].astype(F32) for g in range(2)]
        for g, qr in enumerate(norm_rope(groups, qg_ref[...])):
            qr = qr * scale
            qlo_ref[hp0 + g] = jnp.where(head_a, qr, 0.0).astype(BF16)
            qhi_ref[hp0 + g] = jnp.where(head_a, 0.0, qr).astype(BF16)

    qi = lax.broadcasted_iota(jnp.int32, (2 * WINDOW, WINDOW), 0) % WINDOW
    own_key = lax.broadcasted_iota(jnp.int32, (2 * WINDOW, WINDOW), 1) <= qi
    upper_rows = lax.broadcasted_iota(jnp.int32, (2 * WINDOW, 1), 0) < WINDOW
    out_low = lax.broadcasted_iota(jnp.int32, (WINDOW, LANES), 1) < C_HEAD_DIM

    def head_pair(r0, hp, prev_is_padding):
        kvh = (2 * hp) // C_GROUP
        lhs = jnp.concatenate([qlo_ref[hp, pl.ds(r0, WINDOW), :],
                               qhi_ref[hp, pl.ds(r0, WINDOW), :]], axis=0)
        keys = kd_ref[kvh, pl.ds(r0, 2 * WINDOW), :]
        vals = vd_ref[kvh, pl.ds(r0, 2 * WINDOW), :]
        s = lax.dot_general(lhs, keys, (((1,), (1,)), ((), ())), preferred_element_type=F32)
        s_prev = s[:, :WINDOW]
        if prev_is_padding is not None:
            s_prev = jnp.where(prev_is_padding, NEG_BIG, s_prev)
        s = jnp.where(own_key, s[:, WINDOW:], s_prev)
        sink = jnp.where(upper_rows, sinks_ref[2 * hp] * LOG2_E, sinks_ref[2 * hp + 1] * LOG2_E)
        m = jnp.maximum(jnp.max(s, axis=-1, keepdims=True), sink)
        p = jnp.exp2(s - m)
        p = jnp.concatenate([jnp.where(own_key, 0.0, p), jnp.where(own_key, p, 0.0)], axis=1)
        pv_sum = jnp.dot(p.astype(BF16), vals, preferred_element_type=F32)
        pv = pv_sum[:, :LANES] / (pv_sum[:, LANES:] + jnp.exp2(sink - m))
        o_ref[pl.ds(r0, WINDOW), hp * LANES:(hp + 1) * LANES] = jnp.where(
            out_low, pv[:WINDOW], pv[WINDOW:]).astype(o_ref.dtype)

    steps = [(4.0, kv_prep)]
    steps += [(2.0, functools.partial(q_prep, hp0)) for hp0 in range(0, C_HEADS // 2, 2)]
    for j in range(n_blocks):
        prev_is_padding = first_tile if j == 0 else None
        steps += [(1.0, functools.partial(head_pair, j * WINDOW, hp, prev_is_padding))
                  for hp in range(C_HEADS // 2)]
    return steps


def _attn_mlp_kernel(sinks_ref, q_ref, k_ref, v_ref, cos_ref, sin_ref, qg_ref, kg_ref, seg_ref,
                     x_ref, wo_ref, g_ref, w1_ref, w2_ref, o_ref,
                     qlo_ref, qhi_ref, kd_ref, vd_ref, attn_ref, act_ref, *, tiles_per_seq):
    i = pl.program_id(0)

    @pl.when(i == 0)
    def _():
        attn_ref[...] = jnp.zeros_like(attn_ref)

    _residual_outproj(x_ref, [attn_ref], [wo_ref], o_ref)
    attn_steps = _attn_steps(sinks_ref, q_ref, k_ref, v_ref, cos_ref, sin_ref, qg_ref, kg_ref, seg_ref,
                             attn_ref, qlo_ref, qhi_ref, kd_ref, vd_ref,
                             first_tile=i % tiles_per_seq == 0)
    _run_interleaved(attn_steps, _mlp_steps(g_ref, w1_ref, w2_ref, o_ref, act_ref))


def _attn_mlp(x2, z, cos_t, sin_t, q_gain_tile, k_gain_tile, seg_ones, sinks, w_out_stack, mix_layer,
              gain, w1_stack, w2_stack, layer, seq):
    n, d = x2.shape
    n_tiles = n // T_TAIL
    kv_w = C_KV_HEADS * C_HEAD_DIM
    cur = lambda c: (lambda i: (jnp.minimum(i, n_tiles - 1), c))
    prev = lambda i: (jnp.maximum(i - 1, 0), 0)
    const = lambda i: (0, 0)
    once = pl.Buffered(1)
    return pl.pallas_call(
        functools.partial(_attn_mlp_kernel, tiles_per_seq=seq // T_TAIL),
        out_shape=jax.ShapeDtypeStruct((n, d), F32),
        grid=(n_tiles + 1,),
        in_specs=[pl.BlockSpec(memory_space=pltpu.SMEM),
                  pl.BlockSpec((T_TAIL, D_MODEL), cur(0)),
                  pl.BlockSpec((T_TAIL, kv_w), cur(D_MODEL // kv_w)),
                  pl.BlockSpec((T_TAIL, kv_w), cur(D_MODEL // kv_w + 1)),
                  pl.BlockSpec((T_TAIL, LANES), cur(0)),
                  pl.BlockSpec((T_TAIL, LANES), cur(0)),
                  pl.BlockSpec((1, LANES), const),
                  pl.BlockSpec((1, LANES), const),
                  pl.BlockSpec((2 * LANES, 2 * LANES), const),
                  pl.BlockSpec((T_TAIL, d), prev),
                  pl.BlockSpec((None, d, d), lambda i: (mix_layer, 0, 0), pipeline_mode=once),
                  pl.BlockSpec((1, d), const),
                  pl.BlockSpec((None, d, D_FF), lambda i: (layer, 0, 0), pipeline_mode=once),
                  pl.BlockSpec((None, D_FF, d), lambda i: (layer, 0, 0), pipeline_mode=once)],
        out_specs=pl.BlockSpec((T_TAIL, d), prev),
        scratch_shapes=[pltpu.VMEM((C_HEADS // 2, T_TAIL, LANES), BF16),
                        pltpu.VMEM((C_HEADS // 2, T_TAIL, LANES), BF16),
                        pltpu.VMEM((C_KV_HEADS, T_TAIL + WINDOW, LANES), BF16),
                        pltpu.VMEM((C_KV_HEADS, T_TAIL + WINDOW, 2 * LANES), BF16),
                        pltpu.VMEM((T_TAIL, d), BF16),
                        pltpu.VMEM((T_TAIL, D_FF), BF16)],
        compiler_params=_params(("arbitrary",)),
        name="swa_mlp",
    )(sinks, z, z, z, cos_t, sin_t, q_gain_tile, k_gain_tile, seg_ones,
      x2, w_out_stack, gain, w1_stack, w2_stack)


def _block_diag(w):
    nb, bi, bj = w.shape
    eye = jnp.eye(nb, dtype=w.dtype)
    return (eye[:, None, :, None] * w[:, :, None, :]).reshape(nb * bi, nb * bj)


def _rotary_pair_layout(t):
    lead = t.shape[:-1]
    half = C_HEAD_DIM // 2
    t = t.reshape(*lead, -1, 2, 2, half)
    return jnp.swapaxes(t, -2, -3).reshape(*lead, -1)


def kernel(x, positions, norm_mix, norm_mlp, w_mlp_in, w_mlp_out, w_in_even, w_out_even,
           hgrn_lb_logits, hgrn_out_norm, conv_w, conv_b, rg_wa, rg_ba, rg_wx, rg_bx, rg_lambda,
           w_in_odd, w_out_odd, q_norm, k_norm, sinks):
    bsz, seq, d = x.shape
    n = bsz * seq
    depth = norm_mix.shape[0]
    x2 = x.reshape(n, d)

    half = C_HEAD_DIM // 2
    inv_freq = ROPE_THETA ** (-jnp.arange(0, C_HEAD_DIM, 2, dtype=F32) / C_HEAD_DIM)
    invf_tile = jnp.tile(inv_freq, LANES // half).reshape(1, LANES)
    sign_tile = jnp.concatenate([-jnp.ones((LANES // 2,), F32), jnp.ones((LANES // 2,), F32)]).reshape(1, LANES)
    cos_t, sin_t = _rope_tables(positions.reshape(n, 1), invf_tile, sign_tile)
    lane_head = 2 * (jnp.arange(2 * LANES) // LANES) + (jnp.arange(2 * LANES) // half) % 2
    seg_ones = (lane_head[:, None] == lane_head[None, :]).astype(BF16)

    w_mlp_in_b = w_mlp_in.astype(BF16)
    w_mlp_out_b = w_mlp_out.astype(BF16)
    w_in_even_b = w_in_even.astype(BF16)
    w_out_even_b = w_out_even.astype(BF16)
    q_cols = C_HEADS * C_HEAD_DIM
    qk_cols = q_cols + C_KV_HEADS * C_HEAD_DIM
    w_in_odd_b = jnp.concatenate([_rotary_pair_layout(w_in_odd[..., :q_cols]),
                                  _rotary_pair_layout(w_in_odd[..., q_cols:qk_cols]),
                                  w_in_odd[..., qk_cols:]], axis=-1).astype(BF16)
    w_out_odd_b = w_out_odd.astype(BF16)

    for layer in range(depth):
        gain_mix = norm_mix[layer].reshape(1, d)
        gain_mlp = norm_mlp[layer].reshape(1, d)
        if layer % 2 == 0:
            e = layer // 2
            z = _inproj(x2, gain_mix, w_in_even_b, e, F32)
            ya = _hgrn(z, hgrn_lb_logits, hgrn_out_norm[e].reshape(1, A_WIDTH), e, bsz, seq)
            w_gates = jnp.concatenate([_block_diag(rg_wa[e]), _block_diag(rg_wx[e])], axis=1).astype(BF16)
            b_gates = jnp.concatenate([rg_ba[e], rg_bx[e]]).reshape(1, 2 * B_WIDTH)
            x2 = _rglru_mlp(x2, z, ya, conv_w[e], conv_b[e].reshape(1, B_WIDTH), w_gates, b_gates,
                            rg_lambda[e].reshape(1, B_WIDTH), w_out_even_b, e,
                            gain_mlp, w_mlp_in_b, w_mlp_out_b, layer, seq)
        else:
            o = layer // 2
            z = _inproj(x2, gain_mix, w_in_odd_b, o, BF16)
            q_gain = _rotary_pair_layout(jnp.tile(q_norm[o], LANES // C_HEAD_DIM)).reshape(1, LANES)
            k_gain = _rotary_pair_layout(jnp.tile(k_norm[o], LANES // C_HEAD_DIM)).reshape(1, LANES)
            x2 = _attn_mlp(x2, z, cos_t, sin_t, q_gain, k_gain, seg_ones, sinks[o], w_out_odd_b, o,
                           gain_mlp, w_mlp_in_b, w_mlp_out_b, layer, seq)
    return x2.reshape(bsz, seq, d)
```

```python
import functools

import numpy as np
import jax
import jax.numpy as jnp
from jax import lax
from jax.experimental import pallas as pl
from jax.experimental.pallas import tpu as pltpu

F32 = jnp.float32
BF16 = jnp.bfloat16

D_MODEL = 1024
D_FF = 4 * D_MODEL
RMS_EPS = 1e-6
NEG_BIG = -1e30
F_MIN = 1e-30
A_WIDTH = D_MODEL // 2
A_HEAD_DIM = 128
A_HEADS = A_WIDTH // A_HEAD_DIM
B_WIDTH = D_MODEL // 2
B_BLOCKS = 8
B_BLOCK_DIM = B_WIDTH // B_BLOCKS
B_CONV = 4
RG_C = 8.0
C_HEAD_DIM = 64
C_HEADS = D_MODEL // C_HEAD_DIM
C_KV_HEADS = 4
C_GROUP = C_HEADS // C_KV_HEADS
WINDOW = 128
ROPE_THETA = 10000.0
LOG2_E = 1.4426950408889634
EVEN_IN = 4 * A_WIDTH + 2 * B_WIDTH
ODD_IN = (C_HEADS + 2 * C_KV_HEADS) * C_HEAD_DIM

LANES = 128
SUBLANES = 8
VMEM_LIMIT_BYTES = 56 * 1024 * 1024

TM_PROJ = 1024
T_HGRN = 512
T_TAIL = 512
TAIL_ROW_BLOCK = 256
T_ROPE = 1024
RGLRU_ROW_BLOCK = 128
GATE_SLAB = 256


def _params(semantics):
    return pltpu.CompilerParams(dimension_semantics=semantics, vmem_limit_bytes=VMEM_LIMIT_BYTES)


def _rmsnorm(xf, gain):
    ms = jnp.mean(xf * xf, axis=-1, keepdims=True)
    return xf * lax.rsqrt(ms + RMS_EPS) * gain


def _sigmoid_tanh(x):
    return 0.5 * jnp.tanh(0.5 * x) + 0.5


def _inproj_kernel(x_ref, g_ref, w_ref, o_ref, *, chunk):
    h = _rmsnorm(x_ref[...], g_ref[...]).astype(BF16)
    n_out = o_ref.shape[1]
    for c in range(n_out // chunk):
        cols = slice(c * chunk, (c + 1) * chunk)
        o_ref[:, cols] = jnp.dot(h, w_ref[:, cols], preferred_element_type=F32).astype(o_ref.dtype)


def _inproj(x2, gain, w_stack_bf16, layer, out_dtype):
    n, d = x2.shape
    n_out = w_stack_bf16.shape[2]
    chunk = 512
    return pl.pallas_call(
        functools.partial(_inproj_kernel, chunk=chunk),
        out_shape=jax.ShapeDtypeStruct((n, n_out), out_dtype),
        grid=(n // TM_PROJ,),
        in_specs=[
            pl.BlockSpec((TM_PROJ, d), lambda i: (i, 0)),
            pl.BlockSpec((1, d), lambda i: (0, 0)),
            pl.BlockSpec((None, d, n_out), lambda i: (layer, 0, 0), pipeline_mode=pl.Buffered(1)),
        ],
        out_specs=pl.BlockSpec((TM_PROJ, n_out), lambda i: (i, 0)),
        compiler_params=_params(("parallel",)),
        name="inproj",
    )(x2, gain, w_stack_bf16)


def _residual_outproj(x_ref, mix_refs, wo_refs, o_ref):
    for r0 in range(0, o_ref.shape[0], TAIL_ROW_BLOCK):
        rows = slice(r0, r0 + TAIL_ROW_BLOCK)
        x1 = x_ref[rows, :]
        for m_ref, wo_ref in zip(mix_refs, wo_refs):
            x1 = x1 + jnp.dot(m_ref[rows, :], wo_ref[...], preferred_element_type=F32)
        o_ref[rows, :] = x1


def _mlp_steps(g_ref, w1_ref, w2_ref, o_ref, act_ref):
    held = {}
    up_cols, down_cols = 512, 256
    row_blocks = range(0, o_ref.shape[0], TAIL_ROW_BLOCK)

    def norm(r0):
        rows = slice(r0, r0 + TAIL_ROW_BLOCK)
        held[r0] = _rmsnorm(o_ref[rows, :], g_ref[...]).astype(BF16)

    def up(c, r0):
        rows = slice(r0, r0 + TAIL_ROW_BLOCK)
        cols = slice(c * up_cols, (c + 1) * up_cols)
        u = jnp.maximum(jnp.dot(held[r0], w1_ref[:, cols], preferred_element_type=F32), 0.0)
        act_ref[rows, cols] = (u * u).astype(BF16)

    def down(c, r0):
        rows = slice(r0, r0 + TAIL_ROW_BLOCK)
        cols = slice(c * down_cols, (c + 1) * down_cols)
        o_ref[rows, cols] += jnp.dot(act_ref[rows, :], w2_ref[:, cols], preferred_element_type=F32)

    steps = [(0.1, functools.partial(norm, r0)) for r0 in row_blocks]
    steps += [(0.5, functools.partial(up, c, r0)) for c in range(D_FF // up_cols) for r0 in row_blocks]
    steps += [(0.5, functools.partial(down, c, r0))
              for c in range(D_MODEL // down_cols) for r0 in row_blocks]
    return steps


def _run_interleaved(a_steps, b_steps):
    a_total = sum(c for c, _ in a_steps)
    b_total = sum(c for c, _ in b_steps)
    ia = ib = 0
    a_done = b_done = 0.0
    while ia < len(a_steps) or ib < len(b_steps):
        take_a = ib >= len(b_steps) or (ia < len(a_steps) and a_done / a_total <= b_done / b_total)
        if take_a:
            a_done += a_steps[ia][0]
            a_steps[ia][1]()
            ia += 1
        else:
            b_done += b_steps[ib][0]
            b_steps[ib][1]()
            ib += 1


HGRN_CHUNK = 128
HGRN_LEVELS = (1, 2, 4, 8, 16, 32, 64)
HGRN_SMALL = tuple(m for m in HGRN_LEVELS if m < SUBLANES)


def _hgrn_constants():
    c = HGRN_CHUNK
    t = np.arange(c)[:, None]
    j = np.arange(c)[None, :]
    lower = (j <= t).astype(np.float32)
    blocks = [lower]
    for m in HGRN_SMALL:
        ref = (t // (2 * m)) * (2 * m) + m - 1
        blocks.append(lower - (j <= ref).astype(np.float32))
    cmat = np.concatenate(blocks, axis=0)
    cmat = np.concatenate([cmat, cmat], axis=1)
    masks = [np.eye(c, dtype=np.float32)]
    for m in HGRN_LEVELS:
        same_parent = (t // (2 * m)) == (j // (2 * m))
        masks.append((same_parent & (t % (2 * m) >= m) & (j % (2 * m) < m)).astype(np.float32))
    return cmat, np.stack(masks)


def _hgrn_kernel(q_ref, f_ref, v_ref, g_ref, lbl_ref, gn_ref, cmat_ref, mask_ref, o_ref,
                 st_ref, k_ref, gl_ref, oacc_ref, *, layer_e):
    t_rows = q_ref.shape[0]
    c = HGRN_CHUNK
    contract_lanes = (((1,), (1,)), ((), ()))

    @pl.when(pl.program_id(1) == 0)
    def _():
        st_ref[...] = jnp.zeros_like(st_ref)

    logits = lbl_ref[...]
    ex = jnp.exp(logits - jnp.max(logits, axis=0, keepdims=True))
    sm = ex / jnp.sum(ex, axis=0, keepdims=True)
    lb = jnp.sum(sm[:layer_e + 1], axis=0, keepdims=True) - sm[0:1]

    fx = f_ref[...]
    sig = 1.0 / (1.0 + jnp.exp(-fx))
    gl_ref[...] = jnp.log2(jnp.maximum(lb + (1.0 - lb) * sig, F_MIN))
    k_ref[...] = (1.0 - lb) * (1.0 - sig)

    def exp2_neg_abs(d):
        return jnp.exp2(-jnp.abs(d))

    row8 = lax.broadcasted_iota(jnp.int32, (c, A_HEAD_DIM), 0) % SUBLANES

    states = [st_ref[h] for h in range(A_HEADS)]

    def chunk(ci):
        rows = slice(ci * c, (ci + 1) * c)
        g = gl_ref[rows, :]
        g_hi = g.astype(BF16)
        g_lo = (g - g_hi.astype(F32)).astype(BF16)
        dall = jnp.dot(cmat_ref[...], jnp.concatenate([g_hi, g_lo], axis=0),
                       preferred_element_type=F32)
        for h in range(A_HEADS):
            cols = slice(h * A_HEAD_DIM, (h + 1) * A_HEAD_DIM)
            q = q_ref[rows, cols]
            k = k_ref[rows, cols]
            v = v_ref[rows, cols]
            b = dall[0:c, cols]
            level_x = []
            for li, m in enumerate(HGRN_LEVELS):
                if m < SUBLANES:
                    si = HGRN_SMALL.index(m) + 1
                    d = dall[si * c:(si + 1) * c, cols]
                    qk = jnp.where(row8 % (2 * m) >= m, q, k)
                else:
                    d_parts, qk_parts = [], []
                    for lo in range(0, c, 2 * m):
                        ref = lo + m - 1
                        d_parts.append(b[lo:lo + 2 * m] - b[ref:ref + 1])
                        qk_parts += [k[lo:lo + m], q[lo + m:lo + 2 * m]]
                    d = jnp.concatenate(d_parts, axis=0)
                    qk = jnp.concatenate(qk_parts, axis=0)
                level_x.append(qk * exp2_neg_abs(d))
            diag = jnp.sum(q * k, axis=-1, keepdims=True)
            a_rows = [diag[r0:r0 + SUBLANES] * mask_ref[0, r0:r0 + SUBLANES, :]
                      for r0 in range(0, c, SUBLANES)]
            for li, (m, x) in enumerate(zip(HGRN_LEVELS, level_x)):
                second = [r0 for r0 in range(0, c, SUBLANES) if m < SUBLANES or r0 % (2 * m) >= m]
                x_rows = x if len(second) == len(a_rows) else jnp.concatenate(
                    [x[r0:r0 + SUBLANES] for r0 in second], axis=0)
                gram = jnp.dot(x_rows.astype(BF16), x.T.astype(BF16), preferred_element_type=F32)
                for i, r0 in enumerate(second):
                    a_rows[r0 // SUBLANES] += (gram[i * SUBLANES:(i + 1) * SUBLANES]
                                               * mask_ref[li + 1, r0:r0 + SUBLANES, :])
            a = jnp.concatenate(a_rows, axis=0)
            b_end = b[c - 1:c, :]
            q_dec = (q * jnp.exp2(b)).astype(BF16)
            k_dec = (k * jnp.exp2(b_end - b)).astype(BF16)
            v_t = v.T.astype(BF16)
            st = states[h]
            lhs = jnp.concatenate([a.astype(BF16), q_dec], axis=1)
            rhs_t = jnp.concatenate([v_t, st.astype(BF16)], axis=1)
            oacc_ref[rows, cols] = lax.dot_general(lhs, rhs_t, contract_lanes,
                                                   preferred_element_type=F32)
            states[h] = st * jnp.exp2(b_end) + jnp.dot(v_t, k_dec, preferred_element_type=F32)

    for ci in range(t_rows // c):
        chunk(ci)
    for h in range(A_HEADS):
        st_ref[h] = states[h]

    gn = gn_ref[...]
    for h in range(A_HEADS):
        cols = slice(h * A_HEAD_DIM, (h + 1) * A_HEAD_DIM)
        o = oacc_ref[:, cols]
        on = o * lax.rsqrt(jnp.mean(o * o, axis=-1, keepdims=True) + RMS_EPS) * gn[:, cols]
        g = g_ref[:, cols]
        o_ref[:, cols] = (on * (g * _sigmoid_tanh(g))).astype(o_ref.dtype)


def _hgrn(z, lb_logits, out_norm, layer_e, bsz, seq):
    n = z.shape[0]
    tpb = seq // T_HGRN
    spec = lambda c: pl.BlockSpec((T_HGRN, A_WIDTH), lambda b, t: (b * tpb + t, c))
    const = lambda b, t: (0, 0)
    cmat, masks = _hgrn_constants()
    return pl.pallas_call(
        functools.partial(_hgrn_kernel, layer_e=layer_e),
        out_shape=jax.ShapeDtypeStruct((n, A_WIDTH), BF16),
        grid=(bsz, tpb),
        in_specs=[spec(0), spec(1), spec(2), spec(3),
                  pl.BlockSpec(lb_logits.shape, const),
                  pl.BlockSpec((1, A_WIDTH), const),
                  pl.BlockSpec(cmat.shape, const),
                  pl.BlockSpec(masks.shape, lambda b, t: (0, 0, 0))],
        out_specs=pl.BlockSpec((T_HGRN, A_WIDTH), lambda b, t: (b * tpb + t, 0)),
        scratch_shapes=[pltpu.VMEM((A_HEADS, A_HEAD_DIM, A_HEAD_DIM), F32),
                        pltpu.VMEM((T_HGRN, A_WIDTH), F32),
                        pltpu.VMEM((T_HGRN, A_WIDTH), F32),
                        pltpu.VMEM((T_HGRN, A_WIDTH), F32)],
        compiler_params=_params(("parallel", "arbitrary")),
        name="hgrn2",
    )(z, z, z, z, lb_logits, out_norm, jnp.asarray(cmat, BF16), jnp.asarray(masks, F32))


def _gelu_tanh(x):
    c = 0.7978845608028654
    return x * (0.5 + 0.5 * jnp.tanh(x * (c + (c * 0.044715) * (x * x))))


def _rglru_steps(gate_ref, xb_ref, cw_ref, cb_ref, wg_ref, bg_ref, lam_ref, o_ref,
                 xext_ref, a_ref, u_ref, h_ref, *, first_tile):
    t_rows = xb_ref.shape[0]
    pad = SUBLANES
    rb = RGLRU_ROW_BLOCK
    held = {}

    @pl.when(first_tile)
    def _():
        xext_ref[0:pad, :] = jnp.zeros((pad, B_WIDTH), F32)
        h_ref[...] = jnp.zeros_like(h_ref)

    lam = lam_ref[...]
    softplus_neg_lam = jnp.maximum(-lam, 0.0) + jnp.log1p(jnp.exp(-jnp.abs(lam)))
    log_a_scale = -RG_C * softplus_neg_lam
    cw = cw_ref[...]

    def gates(r0):
        rows = slice(r0, r0 + rb)
        xb = xb_ref[rows, :]
        xext_ref[pad + r0:pad + r0 + rb, :] = xb
        xc = cb_ref[...] + cw[B_CONV - 1:B_CONV, :] * xb
        for j in range(B_CONV - 1):
            shift = B_CONV - 1 - j
            xc = xc + cw[j:j + 1, :] * xext_ref[pad + r0 - shift:pad + r0 - shift + rb, :]
        if r0 + rb == t_rows:
            xext_ref[0:pad, :] = xext_ref[t_rows:t_rows + pad, :]
        xcb = xc.astype(BF16)

        def gate(part):
            slabs = []
            for s0 in range(0, B_WIDTH, GATE_SLAB):
                w = wg_ref[s0:s0 + GATE_SLAB, part * B_WIDTH + s0:part * B_WIDTH + s0 + GATE_SLAB]
                slabs.append(jnp.dot(xcb[:, s0:s0 + GATE_SLAB], w, preferred_element_type=F32))
            pre = jnp.concatenate(slabs, axis=1) + bg_ref[:, part * B_WIDTH:(part + 1) * B_WIDTH]
            return _sigmoid_tanh(pre)

        r = gate(0)
        i = gate(1)
        a = jnp.exp(log_a_scale * r)
        a_ref[rows, :] = a
        y = jnp.maximum(1.0 - a * a, 0.0)
        u_ref[rows, :] = (y * lax.rsqrt(jnp.maximum(y, F_MIN))) * (i * xc)

    row = lax.broadcasted_iota(jnp.int32, (SUBLANES, B_WIDTH), 0)

    def scan(r0):
        h_prev = held.get("h")
        if h_prev is None:
            h_prev = h_ref[...]
        for g0 in range(r0, r0 + rb, SUBLANES):
            rows = slice(g0, g0 + SUBLANES)
            a_cum = a_ref[rows, :]
            h = u_ref[rows, :]
            for sh in (1, 2, 4):
                keep = row >= sh
                h = h + a_cum * jnp.where(keep, pltpu.roll(h, sh, axis=0), 0.0)
                a_cum = a_cum * jnp.where(keep, pltpu.roll(a_cum, sh, axis=0), 1.0)
            h = h + a_cum * h_prev
            u_ref[rows, :] = h
            h_prev = jnp.broadcast_to(h[SUBLANES - 1:SUBLANES, :], (SUBLANES, B_WIDTH))
        held["h"] = h_prev
        if r0 + rb == t_rows:
            h_ref[...] = h_prev

    def gate_out(r0):
        rows = slice(r0, r0 + rb)
        o_ref[rows, :] = (u_ref[rows, :] * _gelu_tanh(gate_ref[rows, :])).astype(o_ref.dtype)

    blocks = range(0, t_rows, rb)
    steps = [(3.0, functools.partial(gates, r0)) for r0 in blocks]
    steps += [(2.0, functools.partial(scan, r0)) for r0 in blocks]
    steps += [(1.0, functools.partial(gate_out, r0)) for r0 in blocks]
    return steps


def _rglru_mlp_kernel(gate_ref, xb_ref, cw_ref, cb_ref, wg_ref, bg_ref, lam_ref,
                      x_ref, ya_ref, woa_ref, wob_ref, g_ref, w1_ref, w2_ref, o_ref,
                      xext_ref, a_ref, u_ref, h_ref, yb_ref, act_ref, *, tiles_per_seq):
    i = pl.program_id(0)

    @pl.when(i == 0)
    def _():
        yb_ref[...] = jnp.zeros_like(yb_ref)

    _residual_outproj(x_ref, [ya_ref, yb_ref], [woa_ref, wob_ref], o_ref)
    rglru_steps = _rglru_steps(gate_ref, xb_ref, cw_ref, cb_ref, wg_ref, bg_ref, lam_ref, yb_ref,
                               xext_ref, a_ref, u_ref, h_ref, first_tile=i % tiles_per_seq == 0)
    _run_interleaved(rglru_steps, _mlp_steps(g_ref, w1_ref, w2_ref, o_ref, act_ref))


def _rglru_mlp(x2, z, ya, conv_w, conv_b, w_gates_bf16, b_gates, lam, w_out_stack, mix_layer,
               gain, w1_stack, w2_stack, layer, seq):
    n, d = x2.shape
    t = T_TAIL
    n_tiles = n // t
    cur = lambda c: (lambda i: (jnp.minimum(i, n_tiles - 1), c))
    prev = lambda i: (jnp.maximum(i - 1, 0), 0)
    const = lambda i: (0, 0)
    once = pl.Buffered(1)
    return pl.pallas_call(
        functools.partial(_rglru_mlp_kernel, tiles_per_seq=seq // t),
        out_shape=jax.ShapeDtypeStruct((n, d), F32),
        grid=(n_tiles + 1,),
        in_specs=[pl.BlockSpec((t, B_WIDTH), cur(4)),
                  pl.BlockSpec((t, B_WIDTH), cur(5)),
                  pl.BlockSpec((B_CONV, B_WIDTH), const),
                  pl.BlockSpec((1, B_WIDTH), const),
                  pl.BlockSpec((B_WIDTH, 2 * B_WIDTH), const),
                  pl.BlockSpec((1, 2 * B_WIDTH), const),
                  pl.BlockSpec((1, B_WIDTH), const),
                  pl.BlockSpec((t, d), prev),
                  pl.BlockSpec((t, A_WIDTH), prev),
                  pl.BlockSpec((None, A_WIDTH, d), lambda i: (mix_layer, 0, 0), pipeline_mode=once),
                  pl.BlockSpec((None, B_WIDTH, d), lambda i: (mix_layer, 1, 0), pipeline_mode=once),
                  pl.BlockSpec((1, d), const),
                  pl.BlockSpec((None, d, D_FF), lambda i: (layer, 0, 0), pipeline_mode=once),
                  pl.BlockSpec((None, D_FF, d), lambda i: (layer, 0, 0), pipeline_mode=once)],
        out_specs=pl.BlockSpec((t, d), prev),
        scratch_shapes=[pltpu.VMEM((t + 2 * SUBLANES, B_WIDTH), F32),
                        pltpu.VMEM((t, B_WIDTH), F32),
                        pltpu.VMEM((t, B_WIDTH), F32),
                        pltpu.VMEM((SUBLANES, B_WIDTH), F32),
                        pltpu.VMEM((t, B_WIDTH), BF16),
                        pltpu.VMEM((t, D_FF), BF16)],
        compiler_params=_params(("arbitrary",)),
        name="rglru_mlp",
    )(z, z, conv_w, conv_b, w_gates_bf16, b_gates, lam, x2, ya, w_out_stack, w_out_stack,
      gain, w1_stack, w2_stack)


ROPE_PACK = LANES // (C_HEAD_DIM // 2)


def _rope_kernel(pos_ref, invf_ref, spread_cos_ref, spread_sin_ref, cos_ref, sin_ref):
    ang = pos_ref[...].astype(F32) * invf_ref[...]
    for val, spread_ref, out_ref in ((jnp.cos(ang), spread_cos_ref, cos_ref),
                                     (jnp.sin(ang), spread_sin_ref, sin_ref)):
        hi = val.astype(BF16)
        lo = (val - hi.astype(F32)).astype(BF16)
        split = jnp.concatenate([hi, lo], axis=1)
        for s in range(ROPE_PACK):
            out_ref[s] = jnp.dot(split, spread_ref[s], preferred_element_type=F32)


def _rope_spread(sign):
    half = C_HEAD_DIM // 2
    src = np.arange(LANES)[:, None]
    dst = np.arange(LANES)[None, :]
    mats = []
    for s in range(ROPE_PACK):
        pick = ((src // half == s) & (src % half == dst % half)).astype(np.float32) * sign[None, :]
        mats.append(np.concatenate([pick, pick], axis=0))
    return np.stack(mats)


def _rope_tables(positions_flat, inv_freq):
    n = positions_flat.shape[0]
    n_rows = n // ROPE_PACK
    half = C_HEAD_DIM // 2
    pos_packed = jnp.repeat(positions_flat.reshape(ROPE_PACK, n_rows).T, half, axis=1)
    invf_tile = jnp.tile(inv_freq, ROPE_PACK).reshape(1, LANES)
    sin_sign = np.where(np.arange(LANES) < LANES // 2, -1.0, 1.0).astype(np.float32)
    const = lambda i: (0, 0)
    const3 = lambda i: (0, 0, 0)
    out_block = pl.BlockSpec((ROPE_PACK, T_ROPE, LANES), lambda i: (0, i, 0))
    cos_t, sin_t = pl.pallas_call(
        _rope_kernel,
        out_shape=(jax.ShapeDtypeStruct((ROPE_PACK, n_rows, LANES), F32),
                   jax.ShapeDtypeStruct((ROPE_PACK, n_rows, LANES), F32)),
        grid=(n_rows // T_ROPE,),
        in_specs=[pl.BlockSpec((T_ROPE, LANES), lambda i: (i, 0)),
                  pl.BlockSpec((1, LANES), const),
                  pl.BlockSpec((ROPE_PACK, 2 * LANES, LANES), const3),
                  pl.BlockSpec((ROPE_PACK, 2 * LANES, LANES), const3)],
        out_specs=(out_block, out_block),
        compiler_params=_params(("parallel",)),
        name="rope_tables",
    )(pos_packed, invf_tile, jnp.asarray(_rope_spread(np.ones(LANES, np.float32)), BF16),
      jnp.asarray(_rope_spread(sin_sign), BF16))
    return cos_t.reshape(n, LANES), sin_t.reshape(n, LANES)


def _attn_steps(sinks_ref, q_ref, k_ref, v_ref, cos_ref, sin_ref, qg_ref, kg_ref, seg_ref, o_ref,
                qlo_ref, qhi_ref, kd_ref, vd_ref, *, first_tile):
    t_rows = q_ref.shape[0]
    n_blocks = t_rows // WINDOW

    half = C_HEAD_DIM // 2
    lane = lax.broadcasted_iota(jnp.int32, (t_rows, LANES), 1)
    low_half = lane < C_HEAD_DIM
    head_a = (lane // half) % 2 == 0
    cos = cos_ref[...]
    sin = sin_ref[...]
    seg = seg_ref[...]

    def norm_rope(xs, gain):
        sq = jnp.concatenate([x * x for x in xs], axis=1).astype(BF16)
        ss = jnp.dot(sq, seg, preferred_element_type=F32)
        out = []
        for g, x in enumerate(xs):
            xn = x * lax.rsqrt(ss[:, g * LANES:(g + 1) * LANES] * (1.0 / C_HEAD_DIM) + RMS_EPS) * gain
            out.append(xn * cos + pltpu.roll(xn, LANES // 2, axis=1) * sin)
        return out

    @pl.when(first_tile)
    def _():
        kd_ref[:, 0:WINDOW, :] = jnp.zeros((C_KV_HEADS, WINDOW, LANES), BF16)
        vd_ref[:, 0:WINDOW, 0:LANES] = jnp.zeros((C_KV_HEADS, WINDOW, LANES), BF16)
        vd_ref[:, :, LANES:] = jnp.ones((C_KV_HEADS, t_rows + WINDOW, LANES), BF16)

    @pl.when(jnp.logical_not(first_tile))
    def _():
        kd_ref[:, 0:WINDOW, :] = kd_ref[:, t_rows:t_rows + WINDOW, :]
        vd_ref[:, 0:WINDOW, :] = vd_ref[:, t_rows:t_rows + WINDOW, :]

    def kv_prep():
        k_groups = norm_rope([k_ref[:, 0:LANES].astype(F32), k_ref[:, LANES:2 * LANES].astype(F32)],
                             kg_ref[...])
        for pair, kr in enumerate(k_groups):
            vr = v_ref[:, pair * LANES:(pair + 1) * LANES].astype(F32)
            v_sw = pltpu.roll(vr, C_HEAD_DIM, axis=1)
            kd_ref[2 * pair, WINDOW:, :] = jnp.where(
                head_a, kr, pltpu.roll(kr, half, axis=1)).astype(BF16)
            kd_ref[2 * pair + 1, WINDOW:, :] = jnp.where(
                head_a, pltpu.roll(kr, LANES - half, axis=1), kr).astype(BF16)
            vd_ref[2 * pair, WINDOW:, 0:LANES] = jnp.where(low_half, vr, v_sw).astype(BF16)
            vd_ref[2 * pair + 1, WINDOW:, 0:LANES] = jnp.where(low_half, v_sw, vr).astype(BF16)

    scale = (C_HEAD_DIM ** -0.5) * LOG2_E

    def q_prep(hp0):
        groups = [q_ref[:, (hp0 + g) * LANES:(hp0 + g + 1) * LANES].astype(F32) for g in range(2)]
        for g, qr in enumerate(norm_rope(groups, qg_ref[...])):
            qr = qr * scale
            qlo_ref[hp0 + g] = jnp.where(head_a, qr, 0.0).astype(BF16)
            qhi_ref[hp0 + g] = jnp.where(head_a, 0.0, qr).astype(BF16)

    qi = lax.broadcasted_iota(jnp.int32, (2 * WINDOW, WINDOW), 0) % WINDOW
    own_key = lax.broadcasted_iota(jnp.int32, (2 * WINDOW, WINDOW), 1) <= qi
    upper_rows = lax.broadcasted_iota(jnp.int32, (2 * WINDOW, 1), 0) < WINDOW
    out_low = lax.broadcasted_iota(jnp.int32, (WINDOW, LANES), 1) < C_HEAD_DIM

    def head_pair(r0, hp, prev_is_padding):
        kvh = (2 * hp) // C_GROUP
        lhs = jnp.concatenate([qlo_ref[hp, pl.ds(r0, WINDOW), :],
                               qhi_ref[hp, pl.ds(r0, WINDOW), :]], axis=0)
        keys = kd_ref[kvh, pl.ds(r0, 2 * WINDOW), :]
        vals = vd_ref[kvh, pl.ds(r0, 2 * WINDOW), :]
        s = lax.dot_general(lhs, keys, (((1,), (1,)), ((), ())), preferred_element_type=F32)
        s_prev = s[:, :WINDOW]
        if prev_is_padding is not None:
            s_prev = jnp.where(prev_is_padding, NEG_BIG, s_prev)
        s = jnp.where(own_key, s[:, WINDOW:], s_prev)
        sink = jnp.where(upper_rows, sinks_ref[2 * hp] * LOG2_E, sinks_ref[2 * hp + 1] * LOG2_E)
        m = jnp.maximum(jnp.max(s, axis=-1, keepdims=True), sink)
        p = jnp.exp2(s - m)
        p = jnp.concatenate([jnp.where(own_key, 0.0, p), jnp.where(own_key, p, 0.0)], axis=1)
        pv_sum = jnp.dot(p.astype(BF16), vals, preferred_element_type=F32)
        pv = pv_sum[:, :LANES] / (pv_sum[:, LANES:] + jnp.exp2(sink - m))
        o_ref[pl.ds(r0, WINDOW), hp * LANES:(hp + 1) * LANES] = jnp.where(
            out_low, pv[:WINDOW], pv[WINDOW:]).astype(o_ref.dtype)

    steps = [(4.0, kv_prep)]
    steps += [(2.0, functools.partial(q_prep, hp0)) for hp0 in range(0, C_HEADS // 2, 2)]
    for j in range(n_blocks):
        prev_is_padding = first_tile if j == 0 else None
        steps += [(1.0, functools.partial(head_pair, j * WINDOW, hp, prev_is_padding))
                  for hp in range(C_HEADS // 2)]
    return steps


def _attn_mlp_kernel(sinks_ref, q_ref, k_ref, v_ref, cos_ref, sin_ref, qg_ref, kg_ref, seg_ref,
                     x_ref, wo_ref, g_ref, w1_ref, w2_ref, o_ref,
                     qlo_ref, qhi_ref, kd_ref, vd_ref, attn_ref, act_ref, *, tiles_per_seq):
    i = pl.program_id(0)

    @pl.when(i == 0)
    def _():
        attn_ref[...] = jnp.zeros_like(attn_ref)

    _residual_outproj(x_ref, [attn_ref], [wo_ref], o_ref)
    attn_steps = _attn_steps(sinks_ref, q_ref, k_ref, v_ref, cos_ref, sin_ref, qg_ref, kg_ref, seg_ref,
                             attn_ref, qlo_ref, qhi_ref, kd_ref, vd_ref,
                             first_tile=i % tiles_per_seq == 0)
    _run_interleaved(attn_steps, _mlp_steps(g_ref, w1_ref, w2_ref, o_ref, act_ref))


def _attn_mlp(x2, z, cos_t, sin_t, q_gain_tile, k_gain_tile, seg_ones, sinks, w_out_stack, mix_layer,
              gain, w1_stack, w2_stack, layer, seq):
    n, d = x2.shape
    n_tiles = n // T_TAIL
    kv_w = C_KV_HEADS * C_HEAD_DIM
    cur = lambda c: (lambda i: (jnp.minimum(i, n_tiles - 1), c))
    prev = lambda i: (jnp.maximum(i - 1, 0), 0)
    const = lambda i: (0, 0)
    once = pl.Buffered(1)
    return pl.pallas_call(
        functools.partial(_attn_mlp_kernel, tiles_per_seq=seq // T_TAIL),
        out_shape=jax.ShapeDtypeStruct((n, d), F32),
        grid=(n_tiles + 1,),
        in_specs=[pl.BlockSpec(memory_space=pltpu.SMEM),
                  pl.BlockSpec((T_TAIL, D_MODEL), cur(0)),
                  pl.BlockSpec((T_TAIL, kv_w), cur(D_MODEL // kv_w)),
                  pl.BlockSpec((T_TAIL, kv_w), cur(D_MODEL // kv_w + 1)),
                  pl.BlockSpec((T_TAIL, LANES), cur(0)),
                  pl.BlockSpec((T_TAIL, LANES), cur(0)),
                  pl.BlockSpec((1, LANES), const),
                  pl.BlockSpec((1, LANES), const),
                  pl.BlockSpec((2 * LANES, 2 * LANES), const),
                  pl.BlockSpec((T_TAIL, d), prev),
                  pl.BlockSpec((None, d, d), lambda i: (mix_layer, 0, 0), pipeline_mode=once),
                  pl.BlockSpec((1, d), const),
                  pl.BlockSpec((None, d, D_FF), lambda i: (layer, 0, 0), pipeline_mode=once),
                  pl.BlockSpec((None, D_FF, d), lambda i: (layer, 0, 0), pipeline_mode=once)],
        out_specs=pl.BlockSpec((T_TAIL, d), prev),
        scratch_shapes=[pltpu.VMEM((C_HEADS // 2, T_TAIL, LANES), BF16),
                        pltpu.VMEM((C_HEADS // 2, T_TAIL, LANES), BF16),
                        pltpu.VMEM((C_KV_HEADS, T_TAIL + WINDOW, LANES), BF16),
                        pltpu.VMEM((C_KV_HEADS, T_TAIL + WINDOW, 2 * LANES), BF16),
                        pltpu.VMEM((T_TAIL, d), BF16),
                        pltpu.VMEM((T_TAIL, D_FF), BF16)],
        compiler_params=_params(("arbitrary",)),
        name="swa_mlp",
    )(sinks, z, z, z, cos_t, sin_t, q_gain_tile, k_gain_tile, seg_ones,
      x2, w_out_stack, gain, w1_stack, w2_stack)


def _block_diag(w):
    nb, bi, bj = w.shape
    eye = jnp.eye(nb, dtype=w.dtype)
    return (eye[:, None, :, None] * w[:, :, None, :]).reshape(nb * bi, nb * bj)


def _rotary_pair_layout(t):
    lead = t.shape[:-1]
    half = C_HEAD_DIM // 2
    t = t.reshape(*lead, -1, 2, 2, half)
    return jnp.swapaxes(t, -2, -3).reshape(*lead, -1)


def kernel(x, positions, norm_mix, norm_mlp, w_mlp_in, w_mlp_out, w_in_even, w_out_even,
           hgrn_lb_logits, hgrn_out_norm, conv_w, conv_b, rg_wa, rg_ba, rg_wx, rg_bx, rg_lambda,
           w_in_odd, w_out_odd, q_norm, k_norm, sinks):
    bsz, seq, d = x.shape
    n = bsz * seq
    depth = norm_mix.shape[0]
    x2 = x.reshape(n, d)

    half = C_HEAD_DIM // 2
    inv_freq = ROPE_THETA ** (-jnp.arange(0, C_HEAD_DIM, 2, dtype=F32) / C_HEAD_DIM)
    cos_t, sin_t = _rope_tables(positions.reshape(n), inv_freq)
    lane_head = 2 * (jnp.arange(2 * LANES) // LANES) + (jnp.arange(2 * LANES) // half) % 2
    seg_ones = (lane_head[:, None] == lane_head[None, :]).astype(BF16)

    w_mlp_in_b = w_mlp_in.astype(BF16)
    w_mlp_out_b = w_mlp_out.astype(BF16)
    w_in_even_b = w_in_even.astype(BF16)
    w_out_even_b = w_out_even.astype(BF16)
    q_cols = C_HEADS * C_HEAD_DIM
    qk_cols = q_cols + C_KV_HEADS * C_HEAD_DIM
    w_in_odd_b = jnp.concatenate([_rotary_pair_layout(w_in_odd[..., :q_cols]),
                                  _rotary_pair_layout(w_in_odd[..., q_cols:qk_cols]),
                                  w_in_odd[..., qk_cols:]], axis=-1).astype(BF16)
    w_out_odd_b = w_out_odd.astype(BF16)

    for layer in range(depth):
        gain_mix = norm_mix[layer].reshape(1, d)
        gain_mlp = norm_mlp[layer].reshape(1, d)
        if layer % 2 == 0:
            e = layer // 2
            z = _inproj(x2, gain_mix, w_in_even_b, e, F32)
            ya = _hgrn(z, hgrn_lb_logits, hgrn_out_norm[e].reshape(1, A_WIDTH), e, bsz, seq)
            w_gates = jnp.concatenate([_block_diag(rg_wa[e]), _block_diag(rg_wx[e])], axis=1).astype(BF16)
            b_gates = jnp.concatenate([rg_ba[e], rg_bx[e]]).reshape(1, 2 * B_WIDTH)
            x2 = _rglru_mlp(x2, z, ya, conv_w[e], conv_b[e].reshape(1, B_WIDTH), w_gates, b_gates,
                            rg_lambda[e].reshape(1, B_WIDTH), w_out_even_b, e,
                            gain_mlp, w_mlp_in_b, w_mlp_out_b, layer, seq)
        else:
            o = layer // 2
            z = _inproj(x2, gain_mix, w_in_odd_b, o, BF16)
            q_gain = _rotary_pair_layout(jnp.tile(q_norm[o], LANES // C_HEAD_DIM)).reshape(1, LANES)
            k_gain = _rotary_pair_layout(jnp.tile(k_norm[o], LANES // C_HEAD_DIM)).reshape(1, LANES)
            x2 = _attn_mlp(x2, z, cos_t, sin_t, q_gain, k_gain, seg_ones, sinks[o], w_out_odd_b, o,
                           gain_mlp, w_mlp_in_b, w_mlp_out_b, layer, seq)
    return x2.reshape(bsz, seq, d)
```

```python
import functools

import numpy as np
import jax
import jax.numpy as jnp
from jax import lax
from jax.experimental import pallas as pl
from jax.experimental.pallas import tpu as pltpu

F32 = jnp.float32
BF16 = jnp.bfloat16

D_MODEL = 1024
D_FF = 4 * D_MODEL
RMS_EPS = 1e-6
NEG_BIG = -1e30
F_MIN = 1e-30
A_WIDTH = D_MODEL // 2
A_HEAD_DIM = 128
A_HEADS = A_WIDTH // A_HEAD_DIM
B_WIDTH = D_MODEL // 2
B_BLOCKS = 8
B_BLOCK_DIM = B_WIDTH // B_BLOCKS
B_CONV = 4
RG_C = 8.0
C_HEAD_DIM = 64
C_HEADS = D_MODEL // C_HEAD_DIM
C_KV_HEADS = 4
C_GROUP = C_HEADS // C_KV_HEADS
WINDOW = 128
ROPE_THETA = 10000.0
LOG2_E = 1.4426950408889634
EVEN_IN = 4 * A_WIDTH + 2 * B_WIDTH
ODD_IN = (C_HEADS + 2 * C_KV_HEADS) * C_HEAD_DIM

LANES = 128
SUBLANES = 8
VMEM_LIMIT_BYTES = 56 * 1024 * 1024

TM_PROJ = 1024
T_HGRN = 512
T_TAIL = 512
TAIL_ROW_BLOCK = 256
T_ROPE = 1024
RGLRU_ROW_BLOCK = 128
GATE_SLAB = 256


def _params(semantics):
    return pltpu.CompilerParams(dimension_semantics=semantics, vmem_limit_bytes=VMEM_LIMIT_BYTES)


def _rmsnorm(xf, gain):
    ms = jnp.mean(xf * xf, axis=-1, keepdims=True)
    return xf * lax.rsqrt(ms + RMS_EPS) * gain


def _sigmoid_tanh(x):
    return 0.5 * jnp.tanh(0.5 * x) + 0.5


def _inproj_kernel(x_ref, g_ref, w_ref, o_ref, *, chunk):
    h = _rmsnorm(x_ref[...], g_ref[...]).astype(BF16)
    n_out = o_ref.shape[1]
    for c in range(n_out // chunk):
        cols = slice(c * chunk, (c + 1) * chunk)
        o_ref[:, cols] = jnp.dot(h, w_ref[:, cols], preferred_element_type=F32).astype(o_ref.dtype)


def _inproj(x2, gain, w_stack_bf16, layer, out_dtype):
    n, d = x2.shape
    n_out = w_stack_bf16.shape[2]
    chunk = 512
    return pl.pallas_call(
        functools.partial(_inproj_kernel, chunk=chunk),
        out_shape=jax.ShapeDtypeStruct((n, n_out), out_dtype),
        grid=(n // TM_PROJ,),
        in_specs=[
            pl.BlockSpec((TM_PROJ, d), lambda i: (i, 0)),
            pl.BlockSpec((1, d), lambda i: (0, 0)),
            pl.BlockSpec((None, d, n_out), lambda i: (layer, 0, 0), pipeline_mode=pl.Buffered(1)),
        ],
        out_specs=pl.BlockSpec((TM_PROJ, n_out), lambda i: (i, 0)),
        compiler_params=_params(("parallel",)),
        name="inproj",
    )(x2, gain, w_stack_bf16)


def _residual_outproj(x_ref, mix_refs, wo_refs, o_ref):
    for r0 in range(0, o_ref.shape[0], TAIL_ROW_BLOCK):
        rows = slice(r0, r0 + TAIL_ROW_BLOCK)
        x1 = x_ref[rows, :]
        for m_ref, wo_ref in zip(mix_refs, wo_refs):
            x1 = x1 + jnp.dot(m_ref[rows, :], wo_ref[...], preferred_element_type=F32)
        o_ref[rows, :] = x1


def _mlp_steps(g_ref, w1_ref, w2_ref, o_ref, act_ref):
    held = {}
    up_cols, down_cols = 1024, 512
    row_blocks = range(0, o_ref.shape[0], TAIL_ROW_BLOCK)

    def norm(r0):
        rows = slice(r0, r0 + TAIL_ROW_BLOCK)
        held[r0] = _rmsnorm(o_ref[rows, :], g_ref[...]).astype(BF16)

    def up(c, r0):
        rows = slice(r0, r0 + TAIL_ROW_BLOCK)
        cols = slice(c * up_cols, (c + 1) * up_cols)
        u = jnp.maximum(jnp.dot(held[r0], w1_ref[:, cols], preferred_element_type=F32), 0.0)
        act_ref[rows, cols] = (u * u).astype(BF16)

    def down(c, r0):
        rows = slice(r0, r0 + TAIL_ROW_BLOCK)
        cols = slice(c * down_cols, (c + 1) * down_cols)
        o_ref[rows, cols] += jnp.dot(act_ref[rows, :], w2_ref[:, cols], preferred_element_type=F32)

    steps = [(0.1, functools.partial(norm, r0)) for r0 in row_blocks]
    steps += [(0.5, functools.partial(up, c, r0)) for c in range(D_FF // up_cols) for r0 in row_blocks]
    steps += [(0.5, functools.partial(down, c, r0))
              for c in range(D_MODEL // down_cols) for r0 in row_blocks]
    return steps


def _run_interleaved(a_steps, b_steps):
    a_total = sum(c for c, _ in a_steps)
    b_total = sum(c for c, _ in b_steps)
    ia = ib = 0
    a_done = b_done = 0.0
    while ia < len(a_steps) or ib < len(b_steps):
        take_a = ib >= len(b_steps) or (ia < len(a_steps) and a_done / a_total <= b_done / b_total)
        if take_a:
            a_done += a_steps[ia][0]
            a_steps[ia][1]()
            ia += 1
        else:
            b_done += b_steps[ib][0]
            b_steps[ib][1]()
            ib += 1


HGRN_CHUNK = 128
HGRN_LEVELS = (1, 2, 4, 8, 16, 32, 64)
HGRN_SMALL = tuple(m for m in HGRN_LEVELS if m < SUBLANES)


def _hgrn_constants():
    c = HGRN_CHUNK
    t = np.arange(c)[:, None]
    j = np.arange(c)[None, :]
    lower = (j <= t).astype(np.float32)
    blocks = [lower]
    for m in HGRN_SMALL:
        ref = (t // (2 * m)) * (2 * m) + m - 1
        blocks.append(lower - (j <= ref).astype(np.float32))
    cmat = np.concatenate(blocks, axis=0)
    cmat = np.concatenate([cmat, cmat], axis=1)
    masks = [np.eye(c, dtype=np.float32)]
    for m in HGRN_LEVELS:
        same_parent = (t // (2 * m)) == (j // (2 * m))
        masks.append((same_parent & (t % (2 * m) >= m) & (j % (2 * m) < m)).astype(np.float32))
    return cmat, np.stack(masks)


def _hgrn_kernel(q_ref, f_ref, v_ref, g_ref, lbl_ref, gn_ref, cmat_ref, mask_ref, o_ref,
                 st_ref, k_ref, gl_ref, oacc_ref, *, layer_e):
    t_rows = q_ref.shape[0]
    c = HGRN_CHUNK
    contract_lanes = (((1,), (1,)), ((), ()))

    @pl.when(pl.program_id(1) == 0)
    def _():
        st_ref[...] = jnp.zeros_like(st_ref)

    logits = lbl_ref[...]
    ex = jnp.exp(logits - jnp.max(logits, axis=0, keepdims=True))
    sm = ex / jnp.sum(ex, axis=0, keepdims=True)
    lb = jnp.sum(sm[:layer_e + 1], axis=0, keepdims=True) - sm[0:1]

    fx = f_ref[...]
    sig = 1.0 / (1.0 + jnp.exp(-fx))
    gl_ref[...] = jnp.log2(jnp.maximum(lb + (1.0 - lb) * sig, F_MIN))
    k_ref[...] = (1.0 - lb) * (1.0 - sig)

    def exp2_neg_abs(d):
        return jnp.exp2(-jnp.abs(d))

    row8 = lax.broadcasted_iota(jnp.int32, (c, A_HEAD_DIM), 0) % SUBLANES

    states = [st_ref[h] for h in range(A_HEADS)]

    def chunk(ci):
        rows = slice(ci * c, (ci + 1) * c)
        g = gl_ref[rows, :]
        g_hi = g.astype(BF16)
        g_lo = (g - g_hi.astype(F32)).astype(BF16)
        dall = jnp.dot(cmat_ref[...], jnp.concatenate([g_hi, g_lo], axis=0),
                       preferred_element_type=F32)
        for h in range(A_HEADS):
            cols = slice(h * A_HEAD_DIM, (h + 1) * A_HEAD_DIM)
            q = q_ref[rows, cols]
            k = k_ref[rows, cols]
            v = v_ref[rows, cols]
            b = dall[0:c, cols]
            level_x = []
            for li, m in enumerate(HGRN_LEVELS):
                if m < SUBLANES:
                    si = HGRN_SMALL.index(m) + 1
                    d = dall[si * c:(si + 1) * c, cols]
                    qk = jnp.where(row8 % (2 * m) >= m, q, k)
                else:
                    d_parts, qk_parts = [], []
                    for lo in range(0, c, 2 * m):
                        ref = lo + m - 1
                        d_parts.append(b[lo:lo + 2 * m] - b[ref:ref + 1])
                        qk_parts += [k[lo:lo + m], q[lo + m:lo + 2 * m]]
                    d = jnp.concatenate(d_parts, axis=0)
                    qk = jnp.concatenate(qk_parts, axis=0)
                level_x.append(qk * exp2_neg_abs(d))
            diag = jnp.sum(q * k, axis=-1, keepdims=True)
            a_rows = [diag[r0:r0 + SUBLANES] * mask_ref[0, r0:r0 + SUBLANES, :]
                      for r0 in range(0, c, SUBLANES)]
            for li, (m, x) in enumerate(zip(HGRN_LEVELS, level_x)):
                second = [r0 for r0 in range(0, c, SUBLANES) if m < SUBLANES or r0 % (2 * m) >= m]
                x_rows = x if len(second) == len(a_rows) else jnp.concatenate(
                    [x[r0:r0 + SUBLANES] for r0 in second], axis=0)
                gram = jnp.dot(x_rows.astype(BF16), x.T.astype(BF16), preferred_element_type=F32)
                for i, r0 in enumerate(second):
                    a_rows[r0 // SUBLANES] += (gram[i * SUBLANES:(i + 1) * SUBLANES]
                                               * mask_ref[li + 1, r0:r0 + SUBLANES, :])
            a = jnp.concatenate(a_rows, axis=0)
            b_end = b[c - 1:c, :]
            q_dec = (q * jnp.exp2(b)).astype(BF16)
            k_dec = (k * jnp.exp2(b_end - b)).astype(BF16)
            v_t = v.T.astype(BF16)
            st = states[h]
            lhs = jnp.concatenate([a.astype(BF16), q_dec], axis=1)
            rhs_t = jnp.concatenate([v_t, st.astype(BF16)], axis=1)
            oacc_ref[rows, cols] = lax.dot_general(lhs, rhs_t, contract_lanes,
                                                   preferred_element_type=F32)
            states[h] = st * jnp.exp2(b_end) + jnp.dot(v_t, k_dec, preferred_element_type=F32)

    for ci in range(t_rows // c):
        chunk(ci)
    for h in range(A_HEADS):
        st_ref[h] = states[h]

    gn = gn_ref[...]
    for h in range(A_HEADS):
        cols = slice(h * A_HEAD_DIM, (h + 1) * A_HEAD_DIM)
        o = oacc_ref[:, cols]
        on = o * lax.rsqrt(jnp.mean(o * o, axis=-1, keepdims=True) + RMS_EPS) * gn[:, cols]
        g = g_ref[:, cols]
        o_ref[:, cols] = (on * (g * _sigmoid_tanh(g))).astype(o_ref.dtype)


def _hgrn(z, lb_logits, out_norm, layer_e, bsz, seq):
    n = z.shape[0]
    tpb = seq // T_HGRN
    spec = lambda c: pl.BlockSpec((T_HGRN, A_WIDTH), lambda b, t: (b * tpb + t, c))
    const = lambda b, t: (0, 0)
    cmat, masks = _hgrn_constants()
    return pl.pallas_call(
        functools.partial(_hgrn_kernel, layer_e=layer_e),
        out_shape=jax.ShapeDtypeStruct((n, A_WIDTH), BF16),
        grid=(bsz, tpb),
        in_specs=[spec(0), spec(1), spec(2), spec(3),
                  pl.BlockSpec(lb_logits.shape, const),
                  pl.BlockSpec((1, A_WIDTH), const),
                  pl.BlockSpec(cmat.shape, const),
                  pl.BlockSpec(masks.shape, lambda b, t: (0, 0, 0))],
        out_specs=pl.BlockSpec((T_HGRN, A_WIDTH), lambda b, t: (b * tpb + t, 0)),
        scratch_shapes=[pltpu.VMEM((A_HEADS, A_HEAD_DIM, A_HEAD_DIM), F32),
                        pltpu.VMEM((T_HGRN, A_WIDTH), F32),
                        pltpu.VMEM((T_HGRN, A_WIDTH), F32),
                        pltpu.VMEM((T_HGRN, A_WIDTH), F32)],
        compiler_params=_params(("parallel", "arbitrary")),
        name="hgrn2",
    )(z, z, z, z, lb_logits, out_norm, jnp.asarray(cmat, BF16), jnp.asarray(masks, F32))


def _gelu_tanh(x):
    c = 0.7978845608028654
    return x * (0.5 + 0.5 * jnp.tanh(x * (c + (c * 0.044715) * (x * x))))


def _rglru_steps(gate_ref, xb_ref, cw_ref, cb_ref, wg_ref, bg_ref, lam_ref, o_ref,
                 xext_ref, a_ref, u_ref, h_ref, *, first_tile):
    t_rows = xb_ref.shape[0]
    pad = SUBLANES
    rb = RGLRU_ROW_BLOCK
    held = {}

    @pl.when(first_tile)
    def _():
        xext_ref[0:pad, :] = jnp.zeros((pad, B_WIDTH), F32)
        h_ref[...] = jnp.zeros_like(h_ref)

    lam = lam_ref[...]
    softplus_neg_lam = jnp.maximum(-lam, 0.0) + jnp.log1p(jnp.exp(-jnp.abs(lam)))
    log_a_scale = -RG_C * softplus_neg_lam
    cw = cw_ref[...]

    def gates(r0):
        rows = slice(r0, r0 + rb)
        xb = xb_ref[rows, :]
        xext_ref[pad + r0:pad + r0 + rb, :] = xb
        xc = cb_ref[...] + cw[B_CONV - 1:B_CONV, :] * xb
        for j in range(B_CONV - 1):
            shift = B_CONV - 1 - j
            xc = xc + cw[j:j + 1, :] * xext_ref[pad + r0 - shift:pad + r0 - shift + rb, :]
        if r0 + rb == t_rows:
            xext_ref[0:pad, :] = xext_ref[t_rows:t_rows + pad, :]
        xcb = xc.astype(BF16)

        def gate(part):
            slabs = []
            for s0 in range(0, B_WIDTH, GATE_SLAB):
                w = wg_ref[s0:s0 + GATE_SLAB, part * B_WIDTH + s0:part * B_WIDTH + s0 + GATE_SLAB]
                slabs.append(jnp.dot(xcb[:, s0:s0 + GATE_SLAB], w, preferred_element_type=F32))
            pre = jnp.concatenate(slabs, axis=1) + bg_ref[:, part * B_WIDTH:(part + 1) * B_WIDTH]
            return _sigmoid_tanh(pre)

        r = gate(0)
        i = gate(1)
        a = jnp.exp(log_a_scale * r)
        a_ref[rows, :] = a
        y = jnp.maximum(1.0 - a * a, 0.0)
        u_ref[rows, :] = (y * lax.rsqrt(jnp.maximum(y, F_MIN))) * (i * xc)

    row = lax.broadcasted_iota(jnp.int32, (SUBLANES, B_WIDTH), 0)

    def scan(r0):
        h_prev = held.get("h")
        if h_prev is None:
            h_prev = h_ref[...]
        for g0 in range(r0, r0 + rb, SUBLANES):
            rows = slice(g0, g0 + SUBLANES)
            a_cum = a_ref[rows, :]
            h = u_ref[rows, :]
            for sh in (1, 2, 4):
                keep = row >= sh
                h = h + a_cum * jnp.where(keep, pltpu.roll(h, sh, axis=0), 0.0)
                a_cum = a_cum * jnp.where(keep, pltpu.roll(a_cum, sh, axis=0), 1.0)
            h = h + a_cum * h_prev
            u_ref[rows, :] = h
            h_prev = jnp.broadcast_to(h[SUBLANES - 1:SUBLANES, :], (SUBLANES, B_WIDTH))
        held["h"] = h_prev
        if r0 + rb == t_rows:
            h_ref[...] = h_prev

    def gate_out(r0):
        rows = slice(r0, r0 + rb)
        o_ref[rows, :] = (u_ref[rows, :] * _gelu_tanh(gate_ref[rows, :])).astype(o_ref.dtype)

    blocks = range(0, t_rows, rb)
    steps = [(3.0, functools.partial(gates, r0)) for r0 in blocks]
    steps += [(2.0, functools.partial(scan, r0)) for r0 in blocks]
    steps += [(1.0, functools.partial(gate_out, r0)) for r0 in blocks]
    return steps


def _rglru_mlp_kernel(gate_ref, xb_ref, cw_ref, cb_ref, wg_ref, bg_ref, lam_ref,
                      x_ref, ya_ref, woa_ref, wob_ref, g_ref, w1_ref, w2_ref, o_ref,
                      xext_ref, a_ref, u_ref, h_ref, yb_ref, act_ref, *, tiles_per_seq):
    i = pl.program_id(0)

    @pl.when(i == 0)
    def _():
        yb_ref[...] = jnp.zeros_like(yb_ref)

    _residual_outproj(x_ref, [ya_ref, yb_ref], [woa_ref, wob_ref], o_ref)
    rglru_steps = _rglru_steps(gate_ref, xb_ref, cw_ref, cb_ref, wg_ref, bg_ref, lam_ref, yb_ref,
                               xext_ref, a_ref, u_ref, h_ref, first_tile=i % tiles_per_seq == 0)
    _run_interleaved(rglru_steps, _mlp_steps(g_ref, w1_ref, w2_ref, o_ref, act_ref))


def _rglru_mlp(x2, z, ya, conv_w, conv_b, w_gates_bf16, b_gates, lam, w_out_stack, mix_layer,
               gain, w1_stack, w2_stack, layer, seq):
    n, d = x2.shape
    t = T_TAIL
    n_tiles = n // t
    cur = lambda c: (lambda i: (jnp.minimum(i, n_tiles - 1), c))
    prev = lambda i: (jnp.maximum(i - 1, 0), 0)
    const = lambda i: (0, 0)
    once = pl.Buffered(1)
    return pl.pallas_call(
        functools.partial(_rglru_mlp_kernel, tiles_per_seq=seq // t),
        out_shape=jax.ShapeDtypeStruct((n, d), F32),
        grid=(n_tiles + 1,),
        in_specs=[pl.BlockSpec((t, B_WIDTH), cur(4)),
                  pl.BlockSpec((t, B_WIDTH), cur(5)),
                  pl.BlockSpec((B_CONV, B_WIDTH), const),
                  pl.BlockSpec((1, B_WIDTH), const),
                  pl.BlockSpec((B_WIDTH, 2 * B_WIDTH), const),
                  pl.BlockSpec((1, 2 * B_WIDTH), const),
                  pl.BlockSpec((1, B_WIDTH), const),
                  pl.BlockSpec((t, d), prev),
                  pl.BlockSpec((t, A_WIDTH), prev),
                  pl.BlockSpec((None, A_WIDTH, d), lambda i: (mix_layer, 0, 0), pipeline_mode=once),
                  pl.BlockSpec((None, B_WIDTH, d), lambda i: (mix_layer, 1, 0), pipeline_mode=once),
                  pl.BlockSpec((1, d), const),
                  pl.BlockSpec((None, d, D_FF), lambda i: (layer, 0, 0), pipeline_mode=once),
                  pl.BlockSpec((None, D_FF, d), lambda i: (layer, 0, 0), pipeline_mode=once)],
        out_specs=pl.BlockSpec((t, d), prev),
        scratch_shapes=[pltpu.VMEM((t + 2 * SUBLANES, B_WIDTH), F32),
                        pltpu.VMEM((t, B_WIDTH), F32),
                        pltpu.VMEM((t, B_WIDTH), F32),
                        pltpu.VMEM((SUBLANES, B_WIDTH), F32),
                        pltpu.VMEM((t, B_WIDTH), BF16),
                        pltpu.VMEM((t, D_FF), BF16)],
        compiler_params=_params(("arbitrary",)),
        name="rglru_mlp",
    )(z, z, conv_w, conv_b, w_gates_bf16, b_gates, lam, x2, ya, w_out_stack, w_out_stack,
      gain, w1_stack, w2_stack)


ROPE_PACK = LANES // (C_HEAD_DIM // 2)


def _rope_kernel(pos_ref, invf_ref, spread_cos_ref, spread_sin_ref, cos_ref, sin_ref):
    ang = pos_ref[...].astype(F32) * invf_ref[...]
    for val, spread_ref, out_ref in ((jnp.cos(ang), spread_cos_ref, cos_ref),
                                     (jnp.sin(ang), spread_sin_ref, sin_ref)):
        hi = val.astype(BF16)
        lo = (val - hi.astype(F32)).astype(BF16)
        split = jnp.concatenate([hi, lo], axis=1)
        for s in range(ROPE_PACK):
            out_ref[s] = jnp.dot(split, spread_ref[s], preferred_element_type=F32)


def _rope_spread(sign):
    half = C_HEAD_DIM // 2
    src = np.arange(LANES)[:, None]
    dst = np.arange(LANES)[None, :]
    mats = []
    for s in range(ROPE_PACK):
        pick = ((src // half == s) & (src % half == dst % half)).astype(np.float32) * sign[None, :]
        mats.append(np.concatenate([pick, pick], axis=0))
    return np.stack(mats)


def _rope_tables(positions_flat, inv_freq):
    n = positions_flat.shape[0]
    n_rows = n // ROPE_PACK
    half = C_HEAD_DIM // 2
    pos_packed = jnp.repeat(positions_flat.reshape(ROPE_PACK, n_rows).T, half, axis=1)
    invf_tile = jnp.tile(inv_freq, ROPE_PACK).reshape(1, LANES)
    sin_sign = np.where(np.arange(LANES) < LANES // 2, -1.0, 1.0).astype(np.float32)
    const = lambda i: (0, 0)
    const3 = lambda i: (0, 0, 0)
    out_block = pl.BlockSpec((ROPE_PACK, T_ROPE, LANES), lambda i: (0, i, 0))
    cos_t, sin_t = pl.pallas_call(
        _rope_kernel,
        out_shape=(jax.ShapeDtypeStruct((ROPE_PACK, n_rows, LANES), F32),
                   jax.ShapeDtypeStruct((ROPE_PACK, n_rows, LANES), F32)),
        grid=(n_rows // T_ROPE,),
        in_specs=[pl.BlockSpec((T_ROPE, LANES), lambda i: (i, 0)),
                  pl.BlockSpec((1, LANES), const),
                  pl.BlockSpec((ROPE_PACK, 2 * LANES, LANES), const3),
                  pl.BlockSpec((ROPE_PACK, 2 * LANES, LANES), const3)],
        out_specs=(out_block, out_block),
        compiler_params=_params(("parallel",)),
        name="rope_tables",
    )(pos_packed, invf_tile, jnp.asarray(_rope_spread(np.ones(LANES, np.float32)), BF16),
      jnp.asarray(_rope_spread(sin_sign), BF16))
    return cos_t.reshape(n, LANES), sin_t.reshape(n, LANES)


def _attn_steps(sinks_ref, q_ref, k_ref, v_ref, cos_ref, sin_ref, qg_ref, kg_ref, seg_ref, o_ref,
                qlo_ref, qhi_ref, kd_ref, vd_ref, *, first_tile):
    t_rows = q_ref.shape[0]
    n_blocks = t_rows // WINDOW

    half = C_HEAD_DIM // 2
    lane = lax.broadcasted_iota(jnp.int32, (t_rows, LANES), 1)
    low_half = lane < C_HEAD_DIM
    head_a = (lane // half) % 2 == 0
    cos = cos_ref[...]
    sin = sin_ref[...]
    seg = seg_ref[...]

    def norm_rope(xs, gain):
        sq = jnp.concatenate([x * x for x in xs], axis=1).astype(BF16)
        ss = jnp.dot(sq, seg, preferred_element_type=F32)
        out = []
        for g, x in enumerate(xs):
            xn = x * lax.rsqrt(ss[:, g * LANES:(g + 1) * LANES] * (1.0 / C_HEAD_DIM) + RMS_EPS) * gain
            out.append(xn * cos + pltpu.roll(xn, LANES // 2, axis=1) * sin)
        return out

    @pl.when(first_tile)
    def _():
        kd_ref[:, 0:WINDOW, :] = jnp.zeros((C_KV_HEADS, WINDOW, LANES), BF16)
        vd_ref[:, 0:WINDOW, 0:LANES] = jnp.zeros((C_KV_HEADS, WINDOW, LANES), BF16)
        vd_ref[:, :, LANES:] = jnp.ones((C_KV_HEADS, t_rows + WINDOW, LANES), BF16)

    @pl.when(jnp.logical_not(first_tile))
    def _():
        kd_ref[:, 0:WINDOW, :] = kd_ref[:, t_rows:t_rows + WINDOW, :]
        vd_ref[:, 0:WINDOW, :] = vd_ref[:, t_rows:t_rows + WINDOW, :]

    def kv_prep():
        k_groups = norm_rope([k_ref[:, 0:LANES].astype(F32), k_ref[:, LANES:2 * LANES].astype(F32)],
                             kg_ref[...])
        for pair, kr in enumerate(k_groups):
            vr = v_ref[:, pair * LANES:(pair + 1) * LANES].astype(F32)
            v_sw = pltpu.roll(vr, C_HEAD_DIM, axis=1)
            kd_ref[2 * pair, WINDOW:, :] = jnp.where(
                head_a, kr, pltpu.roll(kr, half, axis=1)).astype(BF16)
            kd_ref[2 * pair + 1, WINDOW:, :] = jnp.where(
                head_a, pltpu.roll(kr, LANES - half, axis=1), kr).astype(BF16)
            vd_ref[2 * pair, WINDOW:, 0:LANES] = jnp.where(low_half, vr, v_sw).astype(BF16)
            vd_ref[2 * pair + 1, WINDOW:, 0:LANES] = jnp.where(low_half, v_sw, vr).astype(BF16)

    scale = (C_HEAD_DIM ** -0.5) * LOG2_E

    def q_prep(hp0):
        groups = [q_ref[:, (hp0 + g) * LANES:(hp0 + g + 1) * LANES].astype(F32) for g in range(2)]
        for g, qr in enumerate(norm_rope(groups, qg_ref[...])):
            qr = qr * scale
            qlo_ref[hp0 + g] = jnp.where(head_a, qr, 0.0).astype(BF16)
            qhi_ref[hp0 + g] = jnp.where(head_a, 0.0, qr).astype(BF16)

    qi = lax.broadcasted_iota(jnp.int32, (2 * WINDOW, WINDOW), 0) % WINDOW
    own_key = lax.broadcasted_iota(jnp.int32, (2 * WINDOW, WINDOW), 1) <= qi
    upper_rows = lax.broadcasted_iota(jnp.int32, (2 * WINDOW, 1), 0) < WINDOW
    out_low = lax.broadcasted_iota(jnp.int32, (WINDOW, LANES), 1) < C_HEAD_DIM

    def head_pair(r0, hp, prev_is_padding):
        kvh = (2 * hp) // C_GROUP
        lhs = jnp.concatenate([qlo_ref[hp, pl.ds(r0, WINDOW), :],
                               qhi_ref[hp, pl.ds(r0, WINDOW), :]], axis=0)
        keys = kd_ref[kvh, pl.ds(r0, 2 * WINDOW), :]
        vals = vd_ref[kvh, pl.ds(r0, 2 * WINDOW), :]
        s = lax.dot_general(lhs, keys, (((1,), (1,)), ((), ())), preferred_element_type=F32)
        s_prev = s[:, :WINDOW]
        if prev_is_padding is not None:
            s_prev = jnp.where(prev_is_padding, NEG_BIG, s_prev)
        s = jnp.where(own_key, s[:, WINDOW:], s_prev)
        sink = jnp.where(upper_rows, sinks_ref[2 * hp] * LOG2_E, sinks_ref[2 * hp + 1] * LOG2_E)
        m = jnp.maximum(jnp.max(s, axis=-1, keepdims=True), sink)
        p = jnp.exp2(s - m)
        p = jnp.concatenate([jnp.where(own_key, 0.0, p), jnp.where(own_key, p, 0.0)], axis=1)
        pv_sum = jnp.dot(p.astype(BF16), vals, preferred_element_type=F32)
        pv = pv_sum[:, :LANES] / (pv_sum[:, LANES:] + jnp.exp2(sink - m))
        o_ref[pl.ds(r0, WINDOW), hp * LANES:(hp + 1) * LANES] = jnp.where(
            out_low, pv[:WINDOW], pv[WINDOW:]).astype(o_ref.dtype)

    steps = [(4.0, kv_prep)]
    steps += [(2.0, functools.partial(q_prep, hp0)) for hp0 in range(0, C_HEADS // 2, 2)]
    for j in range(n_blocks):
        prev_is_padding = first_tile if j == 0 else None
        steps += [(1.0, functools.partial(head_pair, j * WINDOW, hp, prev_is_padding))
                  for hp in range(C_HEADS // 2)]
    return steps


def _attn_mlp_kernel(sinks_ref, q_ref, k_ref, v_ref, cos_ref, sin_ref, qg_ref, kg_ref, seg_ref,
                     x_ref, wo_ref, g_ref, w1_ref, w2_ref, o_ref,
                     qlo_ref, qhi_ref, kd_ref, vd_ref, attn_ref, act_ref, *, tiles_per_seq):
    i = pl.program_id(0)

    @pl.when(i == 0)
    def _():
        attn_ref[...] = jnp.zeros_like(attn_ref)

    _residual_outproj(x_ref, [attn_ref], [wo_ref], o_ref)
    attn_steps = _attn_steps(sinks_ref, q_ref, k_ref, v_ref, cos_ref, sin_ref, qg_ref, kg_ref, seg_ref,
                             attn_ref, qlo_ref, qhi_ref, kd_ref, vd_ref,
                             first_tile=i % tiles_per_seq == 0)
    _run_interleaved(attn_steps, _mlp_steps(g_ref, w1_ref, w2_ref, o_ref, act_ref))


def _attn_mlp(x2, z, cos_t, sin_t, q_gain_tile, k_gain_tile, seg_ones, sinks, w_out_stack, mix_layer,
              gain, w1_stack, w2_stack, layer, seq):
    n, d = x2.shape
    n_tiles = n // T_TAIL
    kv_w = C_KV_HEADS * C_HEAD_DIM
    cur = lambda c: (lambda i: (jnp.minimum(i, n_tiles - 1), c))
    prev = lambda i: (jnp.maximum(i - 1, 0), 0)
    const = lambda i: (0, 0)
    once = pl.Buffered(1)
    return pl.pallas_call(
        functools.partial(_attn_mlp_kernel, tiles_per_seq=seq // T_TAIL),
        out_shape=jax.ShapeDtypeStruct((n, d), F32),
        grid=(n_tiles + 1,),
        in_specs=[pl.BlockSpec(memory_space=pltpu.SMEM),
                  pl.BlockSpec((T_TAIL, D_MODEL), cur(0)),
                  pl.BlockSpec((T_TAIL, kv_w), cur(D_MODEL // kv_w)),
                  pl.BlockSpec((T_TAIL, kv_w), cur(D_MODEL // kv_w + 1)),
                  pl.BlockSpec((T_TAIL, LANES), cur(0)),
                  pl.BlockSpec((T_TAIL, LANES), cur(0)),
                  pl.BlockSpec((1, LANES), const),
                  pl.BlockSpec((1, LANES), const),
                  pl.BlockSpec((2 * LANES, 2 * LANES), const),
                  pl.BlockSpec((T_TAIL, d), prev),
                  pl.BlockSpec((None, d, d), lambda i: (mix_layer, 0, 0), pipeline_mode=once),
                  pl.BlockSpec((1, d), const),
                  pl.BlockSpec((None, d, D_FF), lambda i: (layer, 0, 0), pipeline_mode=once),
                  pl.BlockSpec((None, D_FF, d), lambda i: (layer, 0, 0), pipeline_mode=once)],
        out_specs=pl.BlockSpec((T_TAIL, d), prev),
        scratch_shapes=[pltpu.VMEM((C_HEADS // 2, T_TAIL, LANES), BF16),
                        pltpu.VMEM((C_HEADS // 2, T_TAIL, LANES), BF16),
                        pltpu.VMEM((C_KV_HEADS, T_TAIL + WINDOW, LANES), BF16),
                        pltpu.VMEM((C_KV_HEADS, T_TAIL + WINDOW, 2 * LANES), BF16),
                        pltpu.VMEM((T_TAIL, d), BF16),
                        pltpu.VMEM((T_TAIL, D_FF), BF16)],
        compiler_params=_params(("arbitrary",)),
        name="swa_mlp",
    )(sinks, z, z, z, cos_t, sin_t, q_gain_tile, k_gain_tile, seg_ones,
      x2, w_out_stack, gain, w1_stack, w2_stack)


def _block_diag(w):
    nb, bi, bj = w.shape
    eye = jnp.eye(nb, dtype=w.dtype)
    return (eye[:, None, :, None] * w[:, :, None, :]).reshape(nb * bi, nb * bj)


def _rotary_pair_layout(t):
    lead = t.shape[:-1]
    half = C_HEAD_DIM // 2
    t = t.reshape(*lead, -1, 2, 2, half)
    return jnp.swapaxes(t, -2, -3).reshape(*lead, -1)


def kernel(x, positions, norm_mix, norm_mlp, w_mlp_in, w_mlp_out, w_in_even, w_out_even,
           hgrn_lb_logits, hgrn_out_norm, conv_w, conv_b, rg_wa, rg_ba, rg_wx, rg_bx, rg_lambda,
           w_in_odd, w_out_odd, q_norm, k_norm, sinks):
    bsz, seq, d = x.shape
    n = bsz * seq
    depth = norm_mix.shape[0]
    x2 = x.reshape(n, d)

    half = C_HEAD_DIM // 2
    inv_freq = ROPE_THETA ** (-jnp.arange(0, C_HEAD_DIM, 2, dtype=F32) / C_HEAD_DIM)
    cos_t, sin_t = _rope_tables(positions.reshape(n), inv_freq)
    lane_head = 2 * (jnp.arange(2 * LANES) // LANES) + (jnp.arange(2 * LANES) // half) % 2
    seg_ones = (lane_head[:, None] == lane_head[None, :]).astype(BF16)

    w_mlp_in_b = w_mlp_in.astype(BF16)
    w_mlp_out_b = w_mlp_out.astype(BF16)
    w_in_even_b = w_in_even.astype(BF16)
    w_out_even_b = w_out_even.astype(BF16)
    q_cols = C_HEADS * C_HEAD_DIM
    qk_cols = q_cols + C_KV_HEADS * C_HEAD_DIM
    w_in_odd_b = jnp.concatenate([_rotary_pair_layout(w_in_odd[..., :q_cols]),
                                  _rotary_pair_layout(w_in_odd[..., q_cols:qk_cols]),
                                  w_in_odd[..., qk_cols:]], axis=-1).astype(BF16)
    w_out_odd_b = w_out_odd.astype(BF16)

    for layer in range(depth):
        gain_mix = norm_mix[layer].reshape(1, d)
        gain_mlp = norm_mlp[layer].reshape(1, d)
        if layer % 2 == 0:
            e = layer // 2
            z = _inproj(x2, gain_mix, w_in_even_b, e, F32)
            ya = _hgrn(z, hgrn_lb_logits, hgrn_out_norm[e].reshape(1, A_WIDTH), e, bsz, seq)
            w_gates = jnp.concatenate([_block_diag(rg_wa[e]), _block_diag(rg_wx[e])], axis=1).astype(BF16)
            b_gates = jnp.concatenate([rg_ba[e], rg_bx[e]]).reshape(1, 2 * B_WIDTH)
            x2 = _rglru_mlp(x2, z, ya, conv_w[e], conv_b[e].reshape(1, B_WIDTH), w_gates, b_gates,
                            rg_lambda[e].reshape(1, B_WIDTH), w_out_even_b, e,
                            gain_mlp, w_mlp_in_b, w_mlp_out_b, layer, seq)
        else:
            o = layer // 2
            z = _inproj(x2, gain_mix, w_in_odd_b, o, BF16)
            q_gain = _rotary_pair_layout(jnp.tile(q_norm[o], LANES // C_HEAD_DIM)).reshape(1, LANES)
            k_gain = _rotary_pair_layout(jnp.tile(k_norm[o], LANES // C_HEAD_DIM)).reshape(1, LANES)
            x2 = _attn_mlp(x2, z, cos_t, sin_t, q_gain, k_gain, seg_ones, sinks[o], w_out_odd_b, o,
                           gain_mlp, w_mlp_in_b, w_mlp_out_b, layer, seq)
    return x2.reshape(bsz, seq, d)
```
